```python
import math
import jax, jax.numpy as jnp
from jax import lax
import numpy as np

D_MODEL = 2048
BATCH = 16
SEQ = 2048
DEPTH = 1

N_Q_HEADS = 16
N_KV_HEADS = 4
Q_GROUP = N_Q_HEADS // N_KV_HEADS
HEAD_DIM = 128
WINDOW = 128
BLOCK = 128
ATTN_WIDTH = N_Q_HEADS * HEAD_DIM
KV_WIDTH = N_KV_HEADS * HEAD_DIM

SSM_WIDTH = D_MODEL // 2
SSM_GROUP = 16
SSM_GROUPS = SSM_WIDTH // SSM_GROUP
SSM_STATE = 64
DT_MIN = 1e-3
DT_MAX = 1e-1

PEER_HEADS = 8
N_KEYS = 128
N_EXPERTS = N_KEYS * N_KEYS
PEER_QDIM = 256
PEER_HALF = PEER_QDIM // 2
PEER_TOPK = 16
PEER_TOKEN_BLOCK = 128

RMS_EPS = 1e-6
IN_WIDTH = ATTN_WIDTH + 2 * KV_WIDTH + SSM_WIDTH + 2 * D_MODEL
IN_SPLITS = (ATTN_WIDTH, ATTN_WIDTH + KV_WIDTH, ATTN_WIDTH + 2 * KV_WIDTH,
             ATTN_WIDTH + 2 * KV_WIDTH + SSM_WIDTH,
             ATTN_WIDTH + 2 * KV_WIDTH + SSM_WIDTH + D_MODEL)

kernel_name = "hybrid_swa_s5_peer_encoder"


def rms_norm(x, g):
    xf = x.astype(jnp.float32)
    y = xf * lax.rsqrt(jnp.mean(xf * xf, axis=-1, keepdims=True) + RMS_EPS)
    return (y * g.astype(jnp.float32)).astype(x.dtype)


def alibi_slopes(n_heads):
    return jnp.exp2(-8.0 * (jnp.arange(n_heads, dtype=jnp.float32) + 1.0) / n_heads)


def windowed_gqa(q, k, v, sink):
    b_, s_ = q.shape[0], q.shape[1]
    nb = s_ // BLOCK
    qb = q.reshape(b_, nb, BLOCK, N_KV_HEADS, Q_GROUP, HEAD_DIM).transpose(1, 0, 2, 3, 4, 5)
    pad = ((0, 0), (BLOCK, BLOCK), (0, 0), (0, 0))

    def band(t):
        tp = jnp.pad(t, pad).reshape(b_, nb + 2, BLOCK, N_KV_HEADS, HEAD_DIM)
        tw = jnp.concatenate([tp[:, :-2], tp[:, 1:-1], tp[:, 2:]], axis=2)
        return tw.transpose(1, 0, 2, 3, 4)

    kw, vw = band(k), band(v)
    slopes = alibi_slopes(N_Q_HEADS).reshape(N_KV_HEADS, Q_GROUP)[:, :, None, None]
    sink_f = sink.astype(jnp.float32).reshape(N_KV_HEADS, Q_GROUP)[:, :, None, None]
    scale = HEAD_DIM ** -0.5

    def block_attn(args):
        blk, qi, ki, vi = args
        qpos = blk * BLOCK + jnp.arange(BLOCK)
        kpos = (blk - 1) * BLOCK + jnp.arange(3 * BLOCK)
        dist = jnp.abs(qpos[:, None] - kpos[None, :])
        valid = (dist <= WINDOW) & (kpos >= 0) & (kpos < s_)
        s = jnp.einsum('bqkgd,bskd->bkgqs', qi.astype(jnp.float32), ki.astype(jnp.float32)) * scale
        s = s - slopes * dist.astype(jnp.float32)
        s = jnp.where(valid, s, -jnp.inf)
        m = jnp.maximum(jnp.max(s, axis=-1, keepdims=True), sink_f)
        p = jnp.exp(s - m)
        denom = jnp.sum(p, axis=-1, keepdims=True) + jnp.exp(sink_f - m)
        o = jnp.einsum('bkgqs,bskd->bqkgd', p / denom, vi.astype(jnp.float32))
        return o.astype(q.dtype)

    out = lax.map(block_attn, (jnp.arange(nb), qb, kw, vw))
    return out.transpose(1, 0, 2, 3, 4, 5).reshape(b_, s_, ATTN_WIDTH)


def _ssm_combine(left, right):
    a_l, b_l = left
    a_r, b_r = right
    return a_r * a_l, a_r * b_l + b_r


def s5_direction(u, a_re, a_im, log_step, b_re, b_im, c_re, c_im, reverse):
    s_ = u.shape[1]
    lam = lax.complex(a_re.astype(jnp.float32), a_im.astype(jnp.float32))
    step = jnp.exp(log_step.astype(jnp.float32))[:, None]
    lam_bar = jnp.exp(lam * step)
    b_mat = lax.complex(b_re.astype(jnp.float32), b_im.astype(jnp.float32))
    b_bar = ((lam_bar - 1.0) / lam)[:, :, None] * b_mat
    bu = jnp.einsum('bsgc,gnc->bsgn', u.astype(jnp.complex64), b_bar)
    a = jnp.broadcast_to(lam_bar[None, None], (1, s_, SSM_GROUPS, SSM_STATE))
    _, states = lax.associative_scan(_ssm_combine, (a, bu), reverse=reverse, axis=1)
    c_mat = lax.complex(c_re.astype(jnp.float32), c_im.astype(jnp.float32))
    return jnp.real(jnp.einsum('bsgn,gcn->bsgc', states, c_mat))


def peer(h, w_query, sub_keys_1, sub_keys_2, expert_down, expert_up):
    b_, s_, d_ = h.shape
    t_ = b_ * s_
    ht = h.reshape(t_, d_)
    q = (ht @ w_query).astype(jnp.float32).reshape(t_, PEER_HEADS, 2, PEER_HALF)
    s1 = jnp.einsum('thd,hkd->thk', q[:, :, 0], sub_keys_1.astype(jnp.float32))
    s2 = jnp.einsum('thd,hkd->thk', q[:, :, 1], sub_keys_2.astype(jnp.float32))
    v1, i1 = lax.top_k(s1, PEER_TOPK)
    v2, i2 = lax.top_k(s2, PEER_TOPK)
    cand = (v1[..., :, None] + v2[..., None, :]).reshape(t_, PEER_HEADS, PEER_TOPK * PEER_TOPK)
    cand_idx = (i1[..., :, None] * N_KEYS + i2[..., None, :]).reshape(t_, PEER_HEADS, PEER_TOPK * PEER_TOPK)
    top_s, pos = lax.top_k(cand, PEER_TOPK)
    idx = jnp.take_along_axis(cand_idx, pos, axis=-1)
    gate = jax.nn.softmax(top_s, axis=-1)
    nblk = t_ // PEER_TOKEN_BLOCK
    k_sel = PEER_HEADS * PEER_TOPK
    xs = (ht.reshape(nblk, PEER_TOKEN_BLOCK, d_),
          idx.reshape(nblk, PEER_TOKEN_BLOCK, k_sel),
          gate.reshape(nblk, PEER_TOKEN_BLOCK, k_sel))

    def token_block(args):
        hb, ib, gb = args
        u_sel = jnp.take(expert_down, ib, axis=0)
        act = jnp.einsum('tkd,td->tk', u_sel.astype(jnp.float32), hb.astype(jnp.float32))
        w = gb * jax.nn.gelu(act, approximate=False)
        v_sel = jnp.take(expert_up, ib, axis=0)
        return jnp.einsum('tk,tkd->td', w, v_sel.astype(jnp.float32))

    return lax.map(token_block, xs).reshape(b_, s_, d_)


def setup_inputs(seed: int = 0) -> dict:
    key = jax.random.key(seed)
    ks = jax.random.split(key, 24)
    L, D, f32 = DEPTH, D_MODEL, jnp.float32
    G, N, C = SSM_GROUPS, SSM_STATE, SSM_GROUP
    nrm = lambda k, shape, sc: jax.random.normal(k, shape, f32) * sc
    a_im_base = jnp.pi * jnp.arange(N, dtype=f32)
    return {
        "x": nrm(ks[0], (BATCH, SEQ, D), 1.0),
        "mix_norm_g": 1.0 + nrm(ks[1], (L, D), 0.02),
        "w_in": nrm(ks[2], (L, D, IN_WIDTH), D ** -0.5),
        "q_norm_g": 1.0 + nrm(ks[3], (L, HEAD_DIM), 0.02),
        "k_norm_g": 1.0 + nrm(ks[4], (L, HEAD_DIM), 0.02),
        "attn_sink": nrm(ks[5], (L, N_Q_HEADS), 0.5),
        "w_attn_o": nrm(ks[6], (L, ATTN_WIDTH, D), ATTN_WIDTH ** -0.5),
        "ssm_a_re": -0.5 + nrm(ks[7], (L, 2, G, N), 0.01),
        "ssm_a_im": a_im_base + nrm(ks[8], (L, 2, G, N), 0.01),
        "ssm_log_step": jax.random.uniform(ks[9], (L, 2, G), f32, math.log(DT_MIN), math.log(DT_MAX)),
        "ssm_b_re": nrm(ks[10], (L, 2, G, N, C), (2.0 * C) ** -0.5),
        "ssm_b_im": nrm(ks[11], (L, 2, G, N, C), (2.0 * C) ** -0.5),
        "ssm_c_re": nrm(ks[12], (L, 2, G, C, N), (2.0 * N) ** -0.5),
        "ssm_c_im": nrm(ks[13], (L, 2, G, C, N), (2.0 * N) ** -0.5),
        "ssm_d": nrm(ks[14], (L, SSM_WIDTH), 1.0),
        "glu_w_a": nrm(ks[15], (L, SSM_WIDTH, D), SSM_WIDTH ** -0.5),
        "glu_w_b": nrm(ks[16], (L, SSM_WIDTH, D), SSM_WIDTH ** -0.5),
        "w_out": nrm(ks[17], (L, D, D), D ** -0.5),
        "ffn_norm_g": 1.0 + nrm(ks[18], (L, D), 0.02),
        "peer_w_query": nrm(ks[19], (L, D, PEER_HEADS * PEER_QDIM), D ** -0.5),
        "peer_sub_keys_1": nrm(ks[20], (L, PEER_HEADS, N_KEYS, PEER_HALF), PEER_HALF ** -0.5),
        "peer_sub_keys_2": nrm(ks[21], (L, PEER_HEADS, N_KEYS, PEER_HALF), PEER_HALF ** -0.5),
        "peer_down": nrm(ks[22], (L, N_EXPERTS, D), D ** -0.5),
        "peer_up": nrm(ks[23], (L, N_EXPERTS, D), 0.25),
    }


def reference(x, mix_norm_g, w_in, q_norm_g, k_norm_g, attn_sink, w_attn_o,
              ssm_a_re, ssm_a_im, ssm_log_step, ssm_b_re, ssm_b_im, ssm_c_re, ssm_c_im, ssm_d,
              glu_w_a, glu_w_b, w_out, ffn_norm_g, peer_w_query, peer_sub_keys_1,
              peer_sub_keys_2, peer_down, peer_up):
    b_, s_ = x.shape[0], x.shape[1]
    for l in range(DEPTH):
        h = rms_norm(x, mix_norm_g[l])
        proj = h @ w_in[l]
        q, k, v, u, gate_a, gate_b = jnp.split(proj, IN_SPLITS, axis=-1)
        q = rms_norm(q.reshape(b_, s_, N_Q_HEADS, HEAD_DIM), q_norm_g[l])
        k = rms_norm(k.reshape(b_, s_, N_KV_HEADS, HEAD_DIM), k_norm_g[l])
        v = v.reshape(b_, s_, N_KV_HEADS, HEAD_DIM)
        y_a = windowed_gqa(q, k, v, attn_sink[l]) @ w_attn_o[l]

        ug = u.astype(jnp.float32).reshape(b_, s_, SSM_GROUPS, SSM_GROUP)
        y_fwd = s5_direction(ug, ssm_a_re[l, 0], ssm_a_im[l, 0], ssm_log_step[l, 0],
                             ssm_b_re[l, 0], ssm_b_im[l, 0], ssm_c_re[l, 0], ssm_c_im[l, 0], False)
        y_bwd = s5_direction(ug, ssm_a_re[l, 1], ssm_a_im[l, 1], ssm_log_step[l, 1],
                             ssm_b_re[l, 1], ssm_b_im[l, 1], ssm_c_re[l, 1], ssm_c_im[l, 1], True)
        y_ssm = (y_fwd + y_bwd).reshape(b_, s_, SSM_WIDTH) + ssm_d[l] * ug.reshape(b_, s_, SSM_WIDTH)
        y_act = jax.nn.gelu(y_ssm, approximate=False)
        y_b = (y_act @ glu_w_a[l]) * jax.nn.sigmoid(y_act @ glu_w_b[l])

        merged = jax.nn.sigmoid(gate_a) * y_a + jax.nn.sigmoid(gate_b) * y_b
        x = x + merged @ w_out[l]

        h2 = rms_norm(x, ffn_norm_g[l])
        x = x + peer(h2, peer_w_query[l], peer_sub_keys_1[l], peer_sub_keys_2[l],
                     peer_down[l], peer_up[l])
    return x
```

```python
import functools
import math

import jax
import jax.numpy as jnp
from jax import lax
from jax.experimental import pallas as pl
from jax.experimental.pallas import tpu as pltpu

N_Q_HEADS = 16
N_KV_HEADS = 4
Q_GROUP = N_Q_HEADS // N_KV_HEADS
HEAD_DIM = 128
WINDOW = 128
BLOCK = 128
SSM_GROUP = 16
SSM_STATE = 64
SSM_CHUNK = 16
PEER_HEADS = 8
N_KEYS = 128
PEER_HALF = 128
PEER_TOPK = 16
RMS_EPS = 1e-6
LANES = 128
VMEM_LIMIT = 56 * 1024 * 1024

_F32 = jnp.float32
_BF16 = jnp.bfloat16
_NT = (((1,), (1,)), ((), ()))


def _params(semantics):
    return pltpu.CompilerParams(dimension_semantics=semantics, vmem_limit_bytes=VMEM_LIMIT)


def _gelu(v):
    return 0.5 * v * (1.0 + lax.erf(v * (1.0 / math.sqrt(2.0))))


def _rms_scale(v):
    return lax.rsqrt(jnp.mean(v * v, axis=-1, keepdims=True) + RMS_EPS)


def _inproj_body(x_ref, g_ref, w_ref, o_ref, h_scr):
    @pl.when(pl.program_id(1) == 0)
    def _():
        x = x_ref[...]
        h_scr[...] = (x * _rms_scale(x) * g_ref[...]).astype(h_scr.dtype)

    o_ref[...] = jnp.dot(h_scr[...], w_ref[...], preferred_element_type=_F32).astype(o_ref.dtype)


def _inproj(x2, gain, w_bf16):
    t, d = x2.shape
    n = w_bf16.shape[1]
    tm, tn = min(512, t), 1024
    return pl.pallas_call(
        _inproj_body,
        out_shape=jax.ShapeDtypeStruct((t, n), _BF16),
        grid=(t // tm, n // tn),
        in_specs=[
            pl.BlockSpec((tm, d), lambda i, j: (i, 0)),
            pl.BlockSpec((1, d), lambda i, j: (0, 0)),
            pl.BlockSpec((d, tn), lambda i, j: (0, j)),
        ],
        out_specs=pl.BlockSpec((tm, tn), lambda i, j: (i, j)),
        scratch_shapes=[pltpu.VMEM((tm, d), _BF16)],
        compiler_params=_params(("arbitrary", "arbitrary")),
        name="inproj",
    )(x2, gain.reshape(1, d), w_bf16)


def _attn_body(sink_ref, q_ref, kp_ref, kc_ref, kn_ref, vp_ref, vc_ref, vn_ref, qg_ref, kg_ref, o_ref,
               *, seq_len, slopes):
    i = pl.program_id(1)
    row = lax.broadcasted_iota(jnp.int32, (BLOCK, 3 * BLOCK), 0)
    col = lax.broadcasted_iota(jnp.int32, (BLOCK, 3 * BLOCK), 1)
    dist = jnp.abs(row + BLOCK - col)
    kpos = (i - 1) * BLOCK + col
    valid = (dist <= WINDOW) & (kpos >= 0) & (kpos < seq_len)
    neg_dist = jnp.where(valid, -dist.astype(_F32), -jnp.inf)

    k3 = jnp.concatenate([kp_ref[...], kc_ref[...], kn_ref[...]], axis=0).astype(_F32)
    v3 = jnp.concatenate([vp_ref[...], vc_ref[...], vn_ref[...]], axis=0)
    qf = q_ref[...].astype(_F32)
    q_gain = qg_ref[...] * (HEAD_DIM ** -0.5)
    k_gain = kg_ref[...]

    for kv in range(N_KV_HEADS):
        kh = k3[:, kv * HEAD_DIM:(kv + 1) * HEAD_DIM]
        kh = (kh * _rms_scale(kh) * k_gain).astype(_BF16)
        q_rows = []
        for g in range(Q_GROUP):
            h = kv * Q_GROUP + g
            qh = qf[:, h * HEAD_DIM:(h + 1) * HEAD_DIM]
            q_rows.append((qh * _rms_scale(qh) * q_gain).astype(_BF16))
        scores = lax.dot_general(jnp.concatenate(q_rows, axis=0), kh, _NT,
                                 preferred_element_type=_F32)
        probs, denoms = [], []
        for g in range(Q_GROUP):
            h = kv * Q_GROUP + g
            s = scores[g * BLOCK:(g + 1) * BLOCK] + slopes[h] * neg_dist
            sink = sink_ref[h]
            m = jnp.maximum(jnp.max(s, axis=-1, keepdims=True), sink)
            p = jnp.exp(s - m)
            denoms.append(jnp.sum(p, axis=-1, keepdims=True) + jnp.exp(sink - m))
            probs.append(p.astype(_BF16))
        pv = jnp.dot(jnp.concatenate(probs, axis=0), v3[:, kv * HEAD_DIM:(kv + 1) * HEAD_DIM],
                     preferred_element_type=_F32)
        for g in range(Q_GROUP):
            h = kv * Q_GROUP + g
            o_ref[:, h * HEAD_DIM:(h + 1) * HEAD_DIM] = (
                pv[g * BLOCK:(g + 1) * BLOCK] / denoms[g]).astype(o_ref.dtype)


def _attention(proj3, q_gain, k_gain, sink):
    b, s, _ = proj3.shape
    nb = s // BLOCK
    aw, kw = N_Q_HEADS * HEAD_DIM, N_KV_HEADS * HEAD_DIM
    k_col, v_col = aw // kw, aw // kw + 1
    slopes = tuple(2.0 ** (-8.0 * (h + 1.0) / N_Q_HEADS) for h in range(N_Q_HEADS))

    def kv_spec(col, shift):
        return pl.BlockSpec((None, BLOCK, kw),
                            lambda bi, i: (bi, jnp.clip(i + shift, 0, nb - 1), col))

    return pl.pallas_call(
        functools.partial(_attn_body, seq_len=s, slopes=slopes),
        out_shape=jax.ShapeDtypeStruct((b, s, aw), _BF16),
        grid=(b, nb),
        in_specs=[
            pl.BlockSpec(memory_space=pltpu.SMEM),
            pl.BlockSpec((None, BLOCK, aw), lambda bi, i: (bi, i, 0)),
            kv_spec(k_col, -1), kv_spec(k_col, 0), kv_spec(k_col, 1),
            kv_spec(v_col, -1), kv_spec(v_col, 0), kv_spec(v_col, 1),
            pl.BlockSpec((1, HEAD_DIM), lambda bi, i: (0, 0)),
            pl.BlockSpec((1, HEAD_DIM), lambda bi, i: (0, 0)),
        ],
        out_specs=pl.BlockSpec((None, BLOCK, aw), lambda bi, i: (bi, i, 0)),
        compiler_params=_params(("arbitrary", "arbitrary")),
        name="attn",
    )(sink.astype(_F32), proj3, proj3, proj3, proj3, proj3, proj3, proj3,
      q_gain.reshape(1, HEAD_DIM).astype(_F32), k_gain.reshape(1, HEAD_DIM).astype(_F32))


def _ssm_matrices(a_re, a_im, log_step, b_re, b_im, c_re, c_im, d_skip):
    hp = lax.Precision.HIGHEST
    n_l, n_c, n_s = SSM_CHUNK, SSM_GROUP, SSM_STATE
    n_g = a_re.shape[1]
    lam = lax.complex(a_re.astype(_F32), a_im.astype(_F32))
    step = jnp.exp(log_step.astype(_F32))[..., None]
    lam_bar = jnp.exp(lam * step)
    b_bar = ((lam_bar - 1.0) / lam)[..., None] * lax.complex(b_re.astype(_F32), b_im.astype(_F32))
    c_mat = lax.complex(c_re.astype(_F32), c_im.astype(_F32))
    steps = jnp.arange(n_l + 1, dtype=_F32)
    pw = jnp.exp((lam * step)[None] * steps[:, None, None, None])

    kern = jnp.real(jnp.einsum('dgcn,tdgn,dgne->dtgce', c_mat, pw[:n_l], b_bar, precision=hp))
    s_idx = jnp.arange(n_l)[:, None]
    t_idx = jnp.arange(n_l)[None, :]
    tau_f, tau_b = t_idx - s_idx, s_idx - t_idx
    resp = (jnp.where((tau_f >= 0)[:, :, None, None, None], kern[0][jnp.clip(tau_f, 0, n_l - 1)], 0.0)
            + jnp.where((tau_b >= 0)[:, :, None, None, None], kern[1][jnp.clip(tau_b, 0, n_l - 1)], 0.0))
    skip = (jnp.eye(n_l, dtype=_F32)[:, :, None, None, None]
            * (jnp.eye(n_c, dtype=_F32)[None, None, None] * d_skip.reshape(1, 1, n_g, n_c, 1)))
    m = (resp + skip).transpose(2, 0, 4, 1, 3).reshape(n_g, n_l * n_c, n_l * n_c)

    def pad_lanes(z):
        return jnp.pad(z, [(0, 0)] * (z.ndim - 1) + [(0, LANES - n_s)])

    w_f = pw[n_l - 1 - jnp.arange(n_l), 0][:, :, None, :] * b_bar[0].transpose(0, 2, 1)[None]
    w_b = pw[jnp.arange(n_l), 1][:, :, None, :] * b_bar[1].transpose(0, 2, 1)[None]
    w_in = jnp.concatenate([pad_lanes(jnp.real(w_f)), pad_lanes(jnp.imag(w_f)),
                            pad_lanes(jnp.real(w_b)), pad_lanes(jnp.imag(w_b))], axis=-1)
    w_in = w_in.transpose(1, 0, 2, 3).reshape(n_g, n_l * n_c, 4 * LANES)

    z_f = c_mat[0].transpose(0, 2, 1)[:, :, None, :] * pw[1 + jnp.arange(n_l), 0].transpose(1, 2, 0)[:, :, :, None]
    z_b = c_mat[1].transpose(0, 2, 1)[:, :, None, :] * pw[n_l - jnp.arange(n_l), 1].transpose(1, 2, 0)[:, :, :, None]

    def pad_rows(z):
        return jnp.pad(z, [(0, 0), (0, LANES - n_s), (0, 0), (0, 0)]).reshape(n_g, LANES, n_l * n_c)

    w_out = jnp.concatenate([pad_rows(jnp.real(z_f)), pad_rows(-jnp.imag(z_f)),
                             pad_rows(jnp.real(z_b)), pad_rows(-jnp.imag(z_b))], axis=1)
    lam_l = pw[n_l]
    lam4 = jnp.stack([pad_lanes(jnp.real(lam_l[0])), pad_lanes(jnp.imag(lam_l[0])),
                      pad_lanes(jnp.real(lam_l[1])), pad_lanes(jnp.imag(lam_l[1]))], axis=1)
    return m.astype(_BF16), w_in.astype(_BF16), w_out.astype(_BF16), lam4


def _ssm_body(u_ref, m_ref, win_ref, wout_ref, lam_ref, o_ref, inc_scr, st_scr, *, rows_per_chunk, n_chunks):
    nb, nk = rows_per_chunk, n_chunks
    u = u_ref[...]
    inc_scr[...] = jnp.dot(u, win_ref[...], preferred_element_type=_F32)
    lam = lam_ref[...]
    fr_a, fi_a, br_a, bi_a = (jnp.broadcast_to(lam[r:r + 1, :], (nb, LANES)) for r in range(4))

    def step(k, carry):
        f_re, f_im, b_re, b_im = carry
        rf = pl.ds(pl.multiple_of(k * nb, nb), nb)
        rb = pl.ds(pl.multiple_of((nk - 1 - k) * nb, nb), nb)
        st_scr[rf, 0 * LANES:1 * LANES] = f_re
        st_scr[rf, 1 * LANES:2 * LANES] = f_im
        st_scr[rb, 2 * LANES:3 * LANES] = b_re
        st_scr[rb, 3 * LANES:4 * LANES] = b_im
        nf_re = fr_a * f_re - fi_a * f_im + inc_scr[rf, 0 * LANES:1 * LANES]
        nf_im = fr_a * f_im + fi_a * f_re + inc_scr[rf, 1 * LANES:2 * LANES]
        nb_re = br_a * b_re - bi_a * b_im + inc_scr[rb, 2 * LANES:3 * LANES]
        nb_im = br_a * b_im + bi_a * b_re + inc_scr[rb, 3 * LANES:4 * LANES]
        return nf_re, nf_im, nb_re, nb_im

    zero = jnp.zeros((nb, LANES), _F32)
    lax.fori_loop(0, nk, step, (zero, zero, zero, zero), unroll=4)
    y = (jnp.dot(u, m_ref[...], preferred_element_type=_F32)
         + jnp.dot(st_scr[...].astype(_BF16), wout_ref[...], preferred_element_type=_F32))
    o_ref[...] = _gelu(y).astype(o_ref.dtype)


def _ssm(u_chunks, m, w_in, w_out, lam4, rows_per_chunk):
    n_g, rows, width = u_chunks.shape
    return pl.pallas_call(
        functools.partial(_ssm_body, rows_per_chunk=rows_per_chunk, n_chunks=rows // rows_per_chunk),
        out_shape=jax.ShapeDtypeStruct((n_g, rows, width), _BF16),
        grid=(n_g,),
        in_specs=[
            pl.BlockSpec((None, rows, width), lambda g: (g, 0, 0)),
            pl.BlockSpec((None, width, width), lambda g: (g, 0, 0)),
            pl.BlockSpec((None, width, 4 * LANES), lambda g: (g, 0, 0)),
            pl.BlockSpec((None, 4 * LANES, width), lambda g: (g, 0, 0)),
            pl.BlockSpec((None, 4, LANES), lambda g: (g, 0, 0)),
        ],
        out_specs=pl.BlockSpec((None, rows, width), lambda g: (g, 0, 0)),
        scratch_shapes=[pltpu.VMEM((rows, 4 * LANES), _F32), pltpu.VMEM((rows, 4 * LANES), _F32)],
        compiler_params=_params(("arbitrary",)),
        name="ssm",
    )(u_chunks, m, w_in, w_out, lam4)


def _merge_body(attn_ref, yact_ref, ga_ref, gb_ref, wo_ref, wa_ref, wb_ref, o_ref):
    y_a = jnp.dot(attn_ref[...], wo_ref[...], preferred_element_type=_F32)
    yact = yact_ref[...]
    glu = (jnp.dot(yact, wa_ref[...], preferred_element_type=_F32)
           * jax.nn.sigmoid(jnp.dot(yact, wb_ref[...], preferred_element_type=_F32)))
    o_ref[...] = (jax.nn.sigmoid(ga_ref[...].astype(_F32)) * y_a
                  + jax.nn.sigmoid(gb_ref[...].astype(_F32)) * glu).astype(o_ref.dtype)


def _merge(attn2, yact2, proj, gate_a_col, gate_b_col, wo, wa, wb):
    t, aw = attn2.shape
    sw = yact2.shape[1]
    d = wo.shape[1]
    tm, tn = min(512, t), 1024
    ga0, gb0 = gate_a_col // tn, gate_b_col // tn
    return pl.pallas_call(
        _merge_body,
        out_shape=jax.ShapeDtypeStruct((t, d), _BF16),
        grid=(t // tm, d // tn),
        in_specs=[
            pl.BlockSpec((tm, aw), lambda i, j: (i, 0)),
            pl.BlockSpec((tm, sw), lambda i, j: (i, 0)),
            pl.BlockSpec((tm, tn), lambda i, j: (i, ga0 + j)),
            pl.BlockSpec((tm, tn), lambda i, j: (i, gb0 + j)),
            pl.BlockSpec((aw, tn), lambda i, j: (0, j)),
            pl.BlockSpec((sw, tn), lambda i, j: (0, j)),
            pl.BlockSpec((sw, tn), lambda i, j: (0, j)),
        ],
        out_specs=pl.BlockSpec((tm, tn), lambda i, j: (i, j)),
        compiler_params=_params(("arbitrary", "arbitrary")),
        name="merge",
    )(attn2, yact2, proj, proj, wo, wa, wb)


def _outproj_body(x_ref, m_ref, w_ref, g_ref, x1_ref, h2_ref):
    x1 = x_ref[...] + jnp.dot(m_ref[...], w_ref[...], preferred_element_type=_F32)
    x1_ref[...] = x1
    h2_ref[...] = (x1 * _rms_scale(x1) * g_ref[...]).astype(h2_ref.dtype)


def _outproj(x2, merged, w_out, gain):
    t, d = x2.shape
    tm = min(512, t)
    return pl.pallas_call(
        _outproj_body,
        out_shape=(jax.ShapeDtypeStruct((t, d), _F32), jax.ShapeDtypeStruct((t, d), _BF16)),
        grid=(t // tm,),
        in_specs=[
            pl.BlockSpec((tm, d), lambda i: (i, 0)),
            pl.BlockSpec((tm, d), lambda i: (i, 0)),
            pl.BlockSpec((d, d), lambda i: (0, 0)),
            pl.BlockSpec((1, d), lambda i: (0, 0)),
        ],
        out_specs=(pl.BlockSpec((tm, d), lambda i: (i, 0)), pl.BlockSpec((tm, d), lambda i: (i, 0))),
        compiler_params=_params(("arbitrary",)),
        name="outproj",
    )(x2, merged, w_out, gain.reshape(1, d))


def _top_rows(vals, k):
    n = vals.shape[0]
    row = lax.broadcasted_iota(jnp.int32, vals.shape, 0)
    rank = jnp.full(vals.shape, k, jnp.int32)
    work = vals
    tops = []
    for r in range(k):
        best = jnp.max(work, axis=0, keepdims=True)
        first = jnp.min(jnp.where(work == best, row, n), axis=0, keepdims=True)
        hit = row == first
        rank = jnp.where(hit, r, rank)
        work = jnp.where(hit, -jnp.inf, work)
        tops.append(best)
    return jnp.concatenate(tops, axis=0), rank


def _select_body(h2_ref, wq_ref, k1_ref, k2_ref, cnt_ref, coef_ref, rank_ref, e2_ref):
    k = PEER_TOPK
    q_t = lax.dot_general(wq_ref[...], h2_ref[...], _NT, preferred_element_type=_F32)
    s1 = jnp.dot(k1_ref[...], q_t[:PEER_HALF].astype(_BF16), preferred_element_type=_F32)
    s2 = jnp.dot(k2_ref[...], q_t[PEER_HALF:].astype(_BF16), preferred_element_type=_F32)
    v1, rank1 = _top_rows(s1, k)
    v2, rank2 = _top_rows(s2, k)
    cand = jnp.concatenate([v1[p:p + 1] + v2 for p in range(k)], axis=0)
    top, rank_c = _top_rows(cand, k)
    chosen = (rank_c < k).astype(_F32)
    count = jnp.zeros(s1.shape, _F32)
    for p in range(k):
        n_p = jnp.sum(chosen[p * k:(p + 1) * k], axis=0, keepdims=True)
        count = jnp.where(rank1 == p, n_p, count)
    z = jnp.sum(jnp.exp(top - top[0:1]), axis=0, keepdims=True)
    cnt_ref[...] = count
    coef_ref[...] = jnp.exp(s1 - v1[0:1]) / z
    rank_ref[...] = rank2.astype(_F32)
    e2_ref[...] = jnp.exp(s2 - v2[0:1])


def _select(h2, wq_t, keys1, keys2):
    t, d = h2.shape
    tm = min(512, t)
    qd = 2 * PEER_HALF
    out = jax.ShapeDtypeStruct((PEER_HEADS, N_KEYS, t), _F32)
    spec = pl.BlockSpec((None, N_KEYS, tm), lambda i, h: (h, 0, i))
    return pl.pallas_call(
        _select_body,
        out_shape=(out, out, out, out),
        grid=(t // tm, PEER_HEADS),
        in_specs=[
            pl.BlockSpec((tm, d), lambda i, h: (i, 0)),
            pl.BlockSpec((qd, d), lambda i, h: (h, 0)),
            pl.BlockSpec((None, N_KEYS, PEER_HALF), lambda i, h: (h, 0, 0)),
            pl.BlockSpec((None, N_KEYS, PEER_HALF), lambda i, h: (h, 0, 0)),
        ],
        out_specs=(spec, spec, spec, spec),
        compiler_params=_params(("arbitrary", "arbitrary")),
        name="select",
    )(h2, wq_t, keys1, keys2)


def _peer_body(h2_ref, down_ref, upt_ref, cnt_ref, coef_ref, rank_ref, e2_ref, x1_ref, o_ref,
               acc_scr, act_scr, wg_scr, *, key_rows):
    j = pl.program_id(1)

    @pl.when(j == 0)
    def _():
        acc_scr[...] = jnp.zeros_like(acc_scr)

    act_scr[...] = lax.dot_general(down_ref[...], h2_ref[...], _NT, preferred_element_type=_F32)
    for a in range(key_rows):
        rows = slice(a * N_KEYS, (a + 1) * N_KEYS)
        w = None
        for h in range(PEER_HEADS):
            term = jnp.where(rank_ref[h] < cnt_ref[h, a:a + 1, :], e2_ref[h], 0.0) * coef_ref[h, a:a + 1, :]
            w = term if w is None else w + term
        wg_scr[rows, :] = (w * _gelu(act_scr[rows, :])).astype(wg_scr.dtype)
    acc_scr[...] += jnp.dot(upt_ref[...], wg_scr[...], preferred_element_type=_F32)

    @pl.when(j == pl.num_programs(1) - 1)
    def _():
        o_ref[...] = x1_ref[...] + acc_scr[...].T


def _peer(h2, down, up_t, count, coef, rank2, e2, x1):
    t, d = h2.shape
    n_e = down.shape[0]
    tm, te = min(512, t), 1024
    key_rows = te // N_KEYS
    once = pl.Buffered(1)
    return pl.pallas_call(
        functools.partial(_peer_body, key_rows=key_rows),
        out_shape=jax.ShapeDtypeStruct((t, d), _F32),
        grid=(t // tm, n_e // te),
        in_specs=[
            pl.BlockSpec((tm, d), lambda i, j: (i, 0), pipeline_mode=once),
            pl.BlockSpec((te, d), lambda i, j: (j, 0)),
            pl.BlockSpec((d, te), lambda i, j: (0, j)),
            pl.BlockSpec((PEER_HEADS, key_rows, tm), lambda i, j: (0, j, i)),
            pl.BlockSpec((PEER_HEADS, key_rows, tm), lambda i, j: (0, j, i)),
            pl.BlockSpec((PEER_HEADS, N_KEYS, tm), lambda i, j: (0, 0, i), pipeline_mode=once),
            pl.BlockSpec((PEER_HEADS, N_KEYS, tm), lambda i, j: (0, 0, i), pipeline_mode=once),
            pl.BlockSpec((tm, d), lambda i, j: (i, 0), pipeline_mode=once),
        ],
        out_specs=pl.BlockSpec((tm, d), lambda i, j: (i, 0)),
        scratch_shapes=[pltpu.VMEM((d, tm), _F32), pltpu.VMEM((te, tm), _F32), pltpu.VMEM((te, tm), _BF16)],
        compiler_params=_params(("arbitrary", "arbitrary")),
        name="peer",
    )(h2, down, up_t, count, coef, rank2, e2, x1)


def _layer(x, mix_g, w_in, q_g, k_g, sink, w_attn_o, a_re, a_im, log_step, b_re, b_im, c_re, c_im, d_skip,
           glu_a, glu_b, w_out, ffn_g, wq, keys1, keys2, down, up):
    b, s, d = x.shape
    t = b * s
    aw, kw = N_Q_HEADS * HEAD_DIM, N_KV_HEADS * HEAD_DIM
    sw = d_skip.shape[0]
    n_g = sw // SSM_GROUP
    u_col = aw + 2 * kw
    ga_col = u_col + sw
    gb_col = ga_col + d
    x2 = x.reshape(t, d)

    proj = _inproj(x2, mix_g, w_in.astype(_BF16))
    attn = _attention(proj.reshape(b, s, -1), q_g, k_g, sink).reshape(t, aw)

    n_k = s // SSM_CHUNK
    u = proj[:, u_col:u_col + sw].reshape(b, n_k, SSM_CHUNK, n_g, SSM_GROUP)
    u_chunks = u.transpose(3, 1, 0, 2, 4).reshape(n_g, n_k * b, SSM_CHUNK * SSM_GROUP)
    m, s_in, s_out, lam4 = _ssm_matrices(a_re, a_im, log_step, b_re, b_im, c_re, c_im, d_skip)
    yact = _ssm(u_chunks, m, s_in, s_out, lam4, b)
    yact = yact.reshape(n_g, n_k, b, SSM_CHUNK, SSM_GROUP).transpose(2, 1, 3, 0, 4).reshape(t, sw)

    merged = _merge(attn, yact, proj, ga_col, gb_col,
                    w_attn_o.astype(_BF16), glu_a.astype(_BF16), glu_b.astype(_BF16))
    x1, h2 = _outproj(x2, merged, w_out.astype(_BF16), ffn_g)

    count, coef, rank2, e2 = _select(h2, wq.T.astype(_BF16), keys1.astype(_BF16), keys2.astype(_BF16))
    out = _peer(h2, down.astype(_BF16), up.T.astype(_BF16), count, coef, rank2, e2, x1)
    return out.reshape(b, s, d)


def kernel(x, mix_norm_g, w_in, q_norm_g, k_norm_g, attn_sink, w_attn_o, ssm_a_re, ssm_a_im, ssm_log_step,
           ssm_b_re, ssm_b_im, ssm_c_re, ssm_c_im, ssm_d, glu_w_a, glu_w_b, w_out, ffn_norm_g, peer_w_query,
           peer_sub_keys_1, peer_sub_keys_2, peer_down, peer_up):
    for l in range(mix_norm_g.shape[0]):
        x = _layer(x, mix_norm_g[l], w_in[l], q_norm_g[l], k_norm_g[l], attn_sink[l], w_attn_o[l],
                   ssm_a_re[l], ssm_a_im[l], ssm_log_step[l], ssm_b_re[l], ssm_b_im[l], ssm_c_re[l],
                   ssm_c_im[l], ssm_d[l], glu_w_a[l], glu_w_b[l], w_out[l], ffn_norm_g[l], peer_w_query[l],
                   peer_sub_keys_1[l], peer_sub_keys_2[l], peer_down[l], peer_up[l])
    return x
```

```python
import functools
import math

import jax
import jax.numpy as jnp
from jax import lax
from jax.experimental import pallas as pl
from jax.experimental.pallas import tpu as pltpu

N_Q_HEADS = 16
N_KV_HEADS = 4
Q_GROUP = N_Q_HEADS // N_KV_HEADS
HEAD_DIM = 128
WINDOW = 128
BLOCK = 128
SSM_GROUP = 16
SSM_STATE = 64
SSM_CHUNK = 16
PEER_HEADS = 8
N_KEYS = 128
PEER_HALF = 128
PEER_TOPK = 16
RMS_EPS = 1e-6
LANES = 128
VMEM_LIMIT = 56 * 1024 * 1024

_F32 = jnp.float32
_BF16 = jnp.bfloat16
_NT = (((1,), (1,)), ((), ()))


def _params(semantics):
    return pltpu.CompilerParams(dimension_semantics=semantics, vmem_limit_bytes=VMEM_LIMIT)


def _gelu(v):
    return 0.5 * v * (1.0 + lax.erf(v * (1.0 / math.sqrt(2.0))))


def _rms_scale(v):
    return lax.rsqrt(jnp.mean(v * v, axis=-1, keepdims=True) + RMS_EPS)


def _inproj_body(x_ref, g_ref, w_ref, o_ref, h_scr):
    @pl.when(pl.program_id(1) == 0)
    def _():
        x = x_ref[...]
        h_scr[...] = (x * _rms_scale(x) * g_ref[...]).astype(h_scr.dtype)

    o_ref[...] = jnp.dot(h_scr[...], w_ref[...], preferred_element_type=_F32).astype(o_ref.dtype)


def _inproj(x2, gain, w_bf16):
    t, d = x2.shape
    n = w_bf16.shape[1]
    tm, tn = min(512, t), 1024
    return pl.pallas_call(
        _inproj_body,
        out_shape=jax.ShapeDtypeStruct((t, n), _BF16),
        grid=(t // tm, n // tn),
        in_specs=[
            pl.BlockSpec((tm, d), lambda i, j: (i, 0)),
            pl.BlockSpec((1, d), lambda i, j: (0, 0)),
            pl.BlockSpec((d, tn), lambda i, j: (0, j)),
        ],
        out_specs=pl.BlockSpec((tm, tn), lambda i, j: (i, j)),
        scratch_shapes=[pltpu.VMEM((tm, d), _BF16)],
        compiler_params=_params(("arbitrary", "arbitrary")),
        name="inproj",
    )(x2, gain.reshape(1, d), w_bf16)


def _attn_body(sink_ref, q_ref, kp_ref, kc_ref, kn_ref, vp_ref, vc_ref, vn_ref, qg_ref, kg_ref, o_ref,
               *, seq_len, slopes):
    i = pl.program_id(1)
    row = lax.broadcasted_iota(jnp.int32, (BLOCK, 3 * BLOCK), 0)
    col = lax.broadcasted_iota(jnp.int32, (BLOCK, 3 * BLOCK), 1)
    dist = jnp.abs(row + BLOCK - col)
    kpos = (i - 1) * BLOCK + col
    valid = (dist <= WINDOW) & (kpos >= 0) & (kpos < seq_len)
    neg_dist = jnp.where(valid, -dist.astype(_F32), -jnp.inf)

    k3 = jnp.concatenate([kp_ref[...], kc_ref[...], kn_ref[...]], axis=0).astype(_F32)
    v3 = jnp.concatenate([vp_ref[...], vc_ref[...], vn_ref[...]], axis=0)
    qf = q_ref[...].astype(_F32)
    q_gain = qg_ref[...] * (HEAD_DIM ** -0.5)
    k_gain = kg_ref[...]

    for kv in range(N_KV_HEADS):
        kh = k3[:, kv * HEAD_DIM:(kv + 1) * HEAD_DIM]
        kh = (kh * _rms_scale(kh) * k_gain).astype(_BF16)
        q_rows = []
        for g in range(Q_GROUP):
            h = kv * Q_GROUP + g
            qh = qf[:, h * HEAD_DIM:(h + 1) * HEAD_DIM]
            q_rows.append((qh * _rms_scale(qh) * q_gain).astype(_BF16))
        scores = lax.dot_general(jnp.concatenate(q_rows, axis=0), kh, _NT,
                                 preferred_element_type=_F32)
        probs, denoms = [], []
        for g in range(Q_GROUP):
            h = kv * Q_GROUP + g
            s = scores[g * BLOCK:(g + 1) * BLOCK] + slopes[h] * neg_dist
            sink = sink_ref[h]
            m = jnp.maximum(jnp.max(s, axis=-1, keepdims=True), sink)
            p = jnp.exp(s - m)
            denoms.append(jnp.sum(p, axis=-1, keepdims=True) + jnp.exp(sink - m))
            probs.append(p.astype(_BF16))
        pv = jnp.dot(jnp.concatenate(probs, axis=0), v3[:, kv * HEAD_DIM:(kv + 1) * HEAD_DIM],
                     preferred_element_type=_F32)
        for g in range(Q_GROUP):
            h = kv * Q_GROUP + g
            o_ref[:, h * HEAD_DIM:(h + 1) * HEAD_DIM] = (
                pv[g * BLOCK:(g + 1) * BLOCK] / denoms[g]).astype(o_ref.dtype)


def _attention(proj3, q_gain, k_gain, sink):
    b, s, _ = proj3.shape
    nb = s // BLOCK
    aw, kw = N_Q_HEADS * HEAD_DIM, N_KV_HEADS * HEAD_DIM
    k_col, v_col = aw // kw, aw // kw + 1
    slopes = tuple(2.0 ** (-8.0 * (h + 1.0) / N_Q_HEADS) for h in range(N_Q_HEADS))

    def kv_spec(col, shift):
        return pl.BlockSpec((None, BLOCK, kw),
                            lambda bi, i: (bi, jnp.clip(i + shift, 0, nb - 1), col))

    return pl.pallas_call(
        functools.partial(_attn_body, seq_len=s, slopes=slopes),
        out_shape=jax.ShapeDtypeStruct((b, s, aw), _BF16),
        grid=(b, nb),
        in_specs=[
            pl.BlockSpec(memory_space=pltpu.SMEM),
            pl.BlockSpec((None, BLOCK, aw), lambda bi, i: (bi, i, 0)),
            kv_spec(k_col, -1), kv_spec(k_col, 0), kv_spec(k_col, 1),
            kv_spec(v_col, -1), kv_spec(v_col, 0), kv_spec(v_col, 1),
            pl.BlockSpec((1, HEAD_DIM), lambda bi, i: (0, 0)),
            pl.BlockSpec((1, HEAD_DIM), lambda bi, i: (0, 0)),
        ],
        out_specs=pl.BlockSpec((None, BLOCK, aw), lambda bi, i: (bi, i, 0)),
        compiler_params=_params(("arbitrary", "arbitrary")),
        name="attn",
    )(sink.astype(_F32), proj3, proj3, proj3, proj3, proj3, proj3, proj3,
      q_gain.reshape(1, HEAD_DIM).astype(_F32), k_gain.reshape(1, HEAD_DIM).astype(_F32))


def _ssm_matrices(a_re, a_im, log_step, b_re, b_im, c_re, c_im, d_skip):
    hp = lax.Precision.HIGHEST
    n_l, n_c, n_s = SSM_CHUNK, SSM_GROUP, SSM_STATE
    n_g = a_re.shape[1]
    a_re, a_im, b_re, b_im, c_re, c_im = (v.astype(_F32) for v in (a_re, a_im, b_re, b_im, c_re, c_im))
    step = jnp.exp(log_step.astype(_F32))[..., None]
    steps = jnp.arange(n_l + 1, dtype=_F32)[:, None, None, None]
    mag = jnp.exp((a_re * step)[None] * steps)
    ang = (a_im * step)[None] * steps
    p_re, p_im = mag * jnp.cos(ang), mag * jnp.sin(ang)
    num_re, num_im = p_re[1] - 1.0, p_im[1]
    den = a_re * a_re + a_im * a_im
    f_re = ((num_re * a_re + num_im * a_im) / den)[..., None]
    f_im = ((num_im * a_re - num_re * a_im) / den)[..., None]
    bb_re, bb_im = f_re * b_re - f_im * b_im, f_re * b_im + f_im * b_re
    pe_re, pe_im = p_re[:, :, :, None, :], p_im[:, :, :, None, :]
    cp_re = c_re[None] * pe_re - c_im[None] * pe_im
    cp_im = c_re[None] * pe_im + c_im[None] * pe_re

    kern = (jnp.einsum('tdgcn,dgne->dtgce', cp_re[:n_l], bb_re, precision=hp)
            - jnp.einsum('tdgcn,dgne->dtgce', cp_im[:n_l], bb_im, precision=hp))
    s_idx = jnp.arange(n_l)[:, None]
    t_idx = jnp.arange(n_l)[None, :]
    tau_f, tau_b = t_idx - s_idx, s_idx - t_idx
    resp = (jnp.where((tau_f >= 0)[:, :, None, None, None], kern[0][jnp.clip(tau_f, 0, n_l - 1)], 0.0)
            + jnp.where((tau_b >= 0)[:, :, None, None, None], kern[1][jnp.clip(tau_b, 0, n_l - 1)], 0.0))
    skip = (jnp.eye(n_l, dtype=_F32)[:, :, None, None, None]
            * (jnp.eye(n_c, dtype=_F32)[None, None, None] * d_skip.astype(_F32).reshape(1, 1, n_g, n_c, 1)))
    m = (resp + skip).transpose(2, 0, 4, 1, 3).reshape(n_g, n_l * n_c, n_l * n_c)

    def pad_lanes(z):
        return jnp.pad(z, [(0, 0)] * (z.ndim - 1) + [(0, LANES - n_s)])

    def state_in(direction, powers):
        q_re, q_im = p_re[powers, direction][:, :, None, :], p_im[powers, direction][:, :, None, :]
        t_re, t_im = bb_re[direction].transpose(0, 2, 1)[None], bb_im[direction].transpose(0, 2, 1)[None]
        return q_re * t_re - q_im * t_im, q_re * t_im + q_im * t_re

    wf_re, wf_im = state_in(0, n_l - 1 - jnp.arange(n_l))
    wb_re, wb_im = state_in(1, jnp.arange(n_l))
    w_in = jnp.concatenate([pad_lanes(wf_re), pad_lanes(wf_im), pad_lanes(wb_re), pad_lanes(wb_im)], axis=-1)
    w_in = w_in.transpose(1, 0, 2, 3).reshape(n_g, n_l * n_c, 4 * LANES)

    def pad_rows(z):
        z = z.transpose(1, 3, 0, 2)
        return jnp.pad(z, [(0, 0), (0, LANES - n_s), (0, 0), (0, 0)]).reshape(n_g, LANES, n_l * n_c)

    out_f, out_b = 1 + jnp.arange(n_l), n_l - jnp.arange(n_l)
    w_out = jnp.concatenate([pad_rows(cp_re[out_f, 0]), pad_rows(-cp_im[out_f, 0]),
                             pad_rows(cp_re[out_b, 1]), pad_rows(-cp_im[out_b, 1])], axis=1)
    lam4 = jnp.stack([pad_lanes(p_re[n_l, 0]), pad_lanes(p_im[n_l, 0]),
                      pad_lanes(p_re[n_l, 1]), pad_lanes(p_im[n_l, 1])], axis=1)
    return m.astype(_BF16), w_in.astype(_BF16), w_out.astype(_BF16), lam4


def _ssm_body(u_ref, m_ref, win_ref, wout_ref, lam_ref, o_ref, inc_scr, st_scr, *, rows_per_chunk, n_chunks):
    nb, nk = rows_per_chunk, n_chunks
    u = u_ref[...]
    inc_scr[...] = jnp.dot(u, win_ref[...], preferred_element_type=_F32)
    lam = lam_ref[...]
    fr_a, fi_a, br_a, bi_a = (jnp.broadcast_to(lam[r:r + 1, :], (nb, LANES)) for r in range(4))

    def step(k, carry):
        f_re, f_im, b_re, b_im = carry
        rf = pl.ds(pl.multiple_of(k * nb, nb), nb)
        rb = pl.ds(pl.multiple_of((nk - 1 - k) * nb, nb), nb)
        st_scr[rf, 0 * LANES:1 * LANES] = f_re
        st_scr[rf, 1 * LANES:2 * LANES] = f_im
        st_scr[rb, 2 * LANES:3 * LANES] = b_re
        st_scr[rb, 3 * LANES:4 * LANES] = b_im
        nf_re = fr_a * f_re - fi_a * f_im + inc_scr[rf, 0 * LANES:1 * LANES]
        nf_im = fr_a * f_im + fi_a * f_re + inc_scr[rf, 1 * LANES:2 * LANES]
        nb_re = br_a * b_re - bi_a * b_im + inc_scr[rb, 2 * LANES:3 * LANES]
        nb_im = br_a * b_im + bi_a * b_re + inc_scr[rb, 3 * LANES:4 * LANES]
        return nf_re, nf_im, nb_re, nb_im

    zero = jnp.zeros((nb, LANES), _F32)
    lax.fori_loop(0, nk, step, (zero, zero, zero, zero), unroll=4)
    y = (jnp.dot(u, m_ref[...], preferred_element_type=_F32)
         + jnp.dot(st_scr[...].astype(_BF16), wout_ref[...], preferred_element_type=_F32))
    o_ref[...] = _gelu(y).astype(o_ref.dtype)


def _ssm(u_chunks, m, w_in, w_out, lam4, rows_per_chunk):
    n_g, rows, width = u_chunks.shape
    return pl.pallas_call(
        functools.partial(_ssm_body, rows_per_chunk=rows_per_chunk, n_chunks=rows // rows_per_chunk),
        out_shape=jax.ShapeDtypeStruct((n_g, rows, width), _BF16),
        grid=(n_g,),
        in_specs=[
            pl.BlockSpec((None, rows, width), lambda g: (g, 0, 0)),
            pl.BlockSpec((None, width, width), lambda g: (g, 0, 0)),
            pl.BlockSpec((None, width, 4 * LANES), lambda g: (g, 0, 0)),
            pl.BlockSpec((None, 4 * LANES, width), lambda g: (g, 0, 0)),
            pl.BlockSpec((None, 4, LANES), lambda g: (g, 0, 0)),
        ],
        out_specs=pl.BlockSpec((None, rows, width), lambda g: (g, 0, 0)),
        scratch_shapes=[pltpu.VMEM((rows, 4 * LANES), _F32), pltpu.VMEM((rows, 4 * LANES), _F32)],
        compiler_params=_params(("arbitrary",)),
        name="ssm",
    )(u_chunks, m, w_in, w_out, lam4)


def _merge_body(attn_ref, yact_ref, ga_ref, gb_ref, wo_ref, wa_ref, wb_ref, o_ref):
    y_a = jnp.dot(attn_ref[...], wo_ref[...], preferred_element_type=_F32)
    yact = yact_ref[...]
    glu = (jnp.dot(yact, wa_ref[...], preferred_element_type=_F32)
           * jax.nn.sigmoid(jnp.dot(yact, wb_ref[...], preferred_element_type=_F32)))
    o_ref[...] = (jax.nn.sigmoid(ga_ref[...].astype(_F32)) * y_a
                  + jax.nn.sigmoid(gb_ref[...].astype(_F32)) * glu).astype(o_ref.dtype)


def _merge(attn2, yact2, proj, gate_a_col, gate_b_col, wo, wa, wb):
    t, aw = attn2.shape
    sw = yact2.shape[1]
    d = wo.shape[1]
    tm, tn = min(512, t), 1024
    ga0, gb0 = gate_a_col // tn, gate_b_col // tn
    return pl.pallas_call(
        _merge_body,
        out_shape=jax.ShapeDtypeStruct((t, d), _BF16),
        grid=(t // tm, d // tn),
        in_specs=[
            pl.BlockSpec((tm, aw), lambda i, j: (i, 0)),
            pl.BlockSpec((tm, sw), lambda i, j: (i, 0)),
            pl.BlockSpec((tm, tn), lambda i, j: (i, ga0 + j)),
            pl.BlockSpec((tm, tn), lambda i, j: (i, gb0 + j)),
            pl.BlockSpec((aw, tn), lambda i, j: (0, j)),
            pl.BlockSpec((sw, tn), lambda i, j: (0, j)),
            pl.BlockSpec((sw, tn), lambda i, j: (0, j)),
        ],
        out_specs=pl.BlockSpec((tm, tn), lambda i, j: (i, j)),
        compiler_params=_params(("arbitrary", "arbitrary")),
        name="merge",
    )(attn2, yact2, proj, proj, wo, wa, wb)


def _outproj_body(x_ref, m_ref, w_ref, g_ref, x1_ref, h2_ref):
    x1 = x_ref[...] + jnp.dot(m_ref[...], w_ref[...], preferred_element_type=_F32)
    x1_ref[...] = x1
    h2_ref[...] = (x1 * _rms_scale(x1) * g_ref[...]).astype(h2_ref.dtype)


def _outproj(x2, merged, w_out, gain):
    t, d = x2.shape
    tm = min(512, t)
    return pl.pallas_call(
        _outproj_body,
        out_shape=(jax.ShapeDtypeStruct((t, d), _F32), jax.ShapeDtypeStruct((t, d), _BF16)),
        grid=(t // tm,),
        in_specs=[
            pl.BlockSpec((tm, d), lambda i: (i, 0)),
            pl.BlockSpec((tm, d), lambda i: (i, 0)),
            pl.BlockSpec((d, d), lambda i: (0, 0)),
            pl.BlockSpec((1, d), lambda i: (0, 0)),
        ],
        out_specs=(pl.BlockSpec((tm, d), lambda i: (i, 0)), pl.BlockSpec((tm, d), lambda i: (i, 0))),
        compiler_params=_params(("arbitrary",)),
        name="outproj",
    )(x2, merged, w_out, gain.reshape(1, d))


_CAND_WIDE = 4
_CAND_TAIL = tuple((p1, p2) for p1 in range(_CAND_WIDE, PEER_TOPK) for p2 in range(PEER_TOPK)
                   if (p1 + 1) * (p2 + 1) <= PEER_TOPK)
_SUBLANES = 8


def _top_rows(vals, k, exact):
    n = vals.shape[0]
    row = lax.broadcasted_iota(jnp.int32, vals.shape, 0).astype(_F32)
    rank = jnp.full(vals.shape, float(k), _F32)
    work = vals
    tops = []
    for r in range(k):
        best = jnp.max(work, axis=0, keepdims=True)
        hit = work == best
        if exact:
            first = jnp.min(jnp.where(hit, row, float(n)), axis=0, keepdims=True)
            hit = row == first
        rank = jnp.where(hit, float(r), rank)
        work = jnp.where(hit, -jnp.inf, work)
        tops.append(best)
    return tops, rank


def _selection(s1, s2, exact):
    k = PEER_TOPK
    t1, rank1 = _top_rows(s1, k, exact)
    t2, rank2 = _top_rows(s2, k, exact)
    v2 = jnp.concatenate(t2, axis=0)
    n_tail = len(_CAND_TAIL)
    pad = [jnp.full_like(t1[0], -jnp.inf)] * (-n_tail % _SUBLANES)
    cand = jnp.concatenate([t1[p] + v2 for p in range(_CAND_WIDE)]
                           + [t1[p1] + t2[p2] for p1, p2 in _CAND_TAIL] + pad, axis=0)
    tc, rank_c = _top_rows(cand, k, exact)
    chosen = jnp.where(rank_c < k, 1.0, 0.0)
    per_rank = [jnp.sum(chosen[p * k:(p + 1) * k], axis=0, keepdims=True) for p in range(_CAND_WIDE)]
    for p1 in range(_CAND_WIDE, k):
        rows = [_CAND_WIDE * k + i for i, (a, _) in enumerate(_CAND_TAIL) if a == p1]
        per_rank.append(jnp.sum(chosen[rows[0]:rows[-1] + 1], axis=0, keepdims=True))
    count = jnp.zeros(s1.shape, _F32)
    for p in range(k):
        count = jnp.where(rank1 == p, per_rank[p], count)
    top = jnp.concatenate(tc, axis=0)
    z = jnp.sum(jnp.exp(top - tc[0]), axis=0, keepdims=True)
    coef = jnp.exp(s1 - t1[0]) / z
    e2 = jnp.exp(s2 - t2[0])

    def excess(rk):
        return jnp.abs(jnp.sum(jnp.where(rk < k, 1.0, 0.0), axis=0, keepdims=True) - k)

    bad = jnp.max(excess(rank1) + excess(rank2) + excess(rank_c))
    return count, coef, rank2, e2, bad


def _select_body(h2_ref, wq_ref, k1_ref, k2_ref, cnt_ref, coef_ref, rank_ref, e2_ref):
    q_t = lax.dot_general(wq_ref[...], h2_ref[...], _NT, preferred_element_type=_F32)
    s1 = jnp.dot(k1_ref[...], q_t[:PEER_HALF].astype(_BF16), preferred_element_type=_F32)
    s2 = jnp.dot(k2_ref[...], q_t[PEER_HALF:].astype(_BF16), preferred_element_type=_F32)

    def emit(count, coef, rank2, e2):
        cnt_ref[...] = count
        coef_ref[...] = coef
        rank_ref[...] = rank2.astype(rank_ref.dtype)
        e2_ref[...] = e2.astype(e2_ref.dtype)

    *fast, bad = _selection(s1, s2, exact=False)
    emit(*fast)

    @pl.when(bad > 0.0)
    def _():
        emit(*_selection(s1, s2, exact=True)[:4])


def _select(h2, wq_t, keys1, keys2):
    t, d = h2.shape
    tm = min(512, t)
    qd = 2 * PEER_HALF
    out = jax.ShapeDtypeStruct((PEER_HEADS, N_KEYS, t), _F32)
    out16 = jax.ShapeDtypeStruct((PEER_HEADS, N_KEYS, t), _BF16)
    spec = pl.BlockSpec((None, N_KEYS, tm), lambda i, h: (h, 0, i))
    return pl.pallas_call(
        _select_body,
        out_shape=(out, out, out16, out16),
        grid=(t // tm, PEER_HEADS),
        in_specs=[
            pl.BlockSpec((tm, d), lambda i, h: (i, 0)),
            pl.BlockSpec((qd, d), lambda i, h: (h, 0)),
            pl.BlockSpec((None, N_KEYS, PEER_HALF), lambda i, h: (h, 0, 0)),
            pl.BlockSpec((None, N_KEYS, PEER_HALF), lambda i, h: (h, 0, 0)),
        ],
        out_specs=(spec, spec, spec, spec),
        compiler_params=_params(("arbitrary", "arbitrary")),
        name="select",
    )(h2, wq_t, keys1, keys2)


def _peer_body(h2_ref, down_ref, upt_ref, cnt_ref, coef_ref, rank_ref, e2_ref, x1_ref, o_ref,
               acc_scr, wg_scr, *, key_rows):
    j = pl.program_id(1)

    @pl.when(j == 0)
    def _():
        acc_scr[...] = jnp.zeros_like(acc_scr)

    act = lax.dot_general(down_ref[...], h2_ref[...], _NT, preferred_element_type=_F32)
    gact = _gelu(act).astype(_BF16)
    block = (N_KEYS, gact.shape[1])
    zero = jnp.zeros(block, _BF16)
    for a in range(key_rows):
        rows = slice(a * N_KEYS, (a + 1) * N_KEYS)
        w = None
        for h in range(PEER_HEADS):
            cnt = jnp.broadcast_to(cnt_ref[h, a:a + 1, :].astype(_BF16), block)
            coef = jnp.broadcast_to(coef_ref[h, a:a + 1, :].astype(_BF16), block)
            term = jnp.where(rank_ref[h] < cnt, e2_ref[h], zero) * coef
            w = term if w is None else w + term
        wg_scr[rows, :] = w * gact[rows, :]
    acc_scr[...] += jnp.dot(upt_ref[...], wg_scr[...], preferred_element_type=_F32)

    @pl.when(j == pl.num_programs(1) - 1)
    def _():
        o_ref[...] = x1_ref[...] + acc_scr[...].T


def _peer(h2, down, up_t, count, coef, rank2, e2, x1):
    t, d = h2.shape
    n_e = down.shape[0]
    tm, te = min(512, t), 1024
    key_rows = te // N_KEYS
    once = pl.Buffered(1)
    return pl.pallas_call(
        functools.partial(_peer_body, key_rows=key_rows),
        out_shape=jax.ShapeDtypeStruct((t, d), _F32),
        grid=(t // tm, n_e // te),
        in_specs=[
            pl.BlockSpec((tm, d), lambda i, j: (i, 0), pipeline_mode=once),
            pl.BlockSpec((te, d), lambda i, j: (j, 0)),
            pl.BlockSpec((d, te), lambda i, j: (0, j)),
            pl.BlockSpec((PEER_HEADS, key_rows, tm), lambda i, j: (0, j, i)),
            pl.BlockSpec((PEER_HEADS, key_rows, tm), lambda i, j: (0, j, i)),
            pl.BlockSpec((PEER_HEADS, N_KEYS, tm), lambda i, j: (0, 0, i), pipeline_mode=once),
            pl.BlockSpec((PEER_HEADS, N_KEYS, tm), lambda i, j: (0, 0, i), pipeline_mode=once),
            pl.BlockSpec((tm, d), lambda i, j: (i, 0), pipeline_mode=once),
        ],
        out_specs=pl.BlockSpec((tm, d), lambda i, j: (i, 0)),
        scratch_shapes=[pltpu.VMEM((d, tm), _F32), pltpu.VMEM((te, tm), _BF16)],
        compiler_params=_params(("arbitrary", "arbitrary")),
        name="peer",
    )(h2, down, up_t, count, coef, rank2, e2, x1)


def _layer(x, mix_g, w_in, q_g, k_g, sink, w_attn_o, a_re, a_im, log_step, b_re, b_im, c_re, c_im, d_skip,
           glu_a, glu_b, w_out, ffn_g, wq, keys1, keys2, down, up):
    b, s, d = x.shape
    t = b * s
    aw, kw = N_Q_HEADS * HEAD_DIM, N_KV_HEADS * HEAD_DIM
    sw = d_skip.shape[0]
    n_g = sw // SSM_GROUP
    u_col = aw + 2 * kw
    ga_col = u_col + sw
    gb_col = ga_col + d
    x2 = x.reshape(t, d)

    proj = _inproj(x2, mix_g, w_in.astype(_BF16))
    attn = _attention(proj.reshape(b, s, -1), q_g, k_g, sink).reshape(t, aw)

    n_k = s // SSM_CHUNK
    u = proj[:, u_col:u_col + sw].reshape(b, n_k, SSM_CHUNK, n_g, SSM_GROUP)
    u_chunks = u.transpose(3, 1, 0, 2, 4).reshape(n_g, n_k * b, SSM_CHUNK * SSM_GROUP)
    m, s_in, s_out, lam4 = _ssm_matrices(a_re, a_im, log_step, b_re, b_im, c_re, c_im, d_skip)
    yact = _ssm(u_chunks, m, s_in, s_out, lam4, b)
    yact = yact.reshape(n_g, n_k, b, SSM_CHUNK, SSM_GROUP).transpose(2, 1, 3, 0, 4).reshape(t, sw)

    merged = _merge(attn, yact, proj, ga_col, gb_col,
                    w_attn_o.astype(_BF16), glu_a.astype(_BF16), glu_b.astype(_BF16))
    x1, h2 = _outproj(x2, merged, w_out.astype(_BF16), ffn_g)

    count, coef, rank2, e2 = _select(h2, wq.T.astype(_BF16), keys1.astype(_BF16), keys2.astype(_BF16))
    out = _peer(h2, down.astype(_BF16), up.T.astype(_BF16), count, coef, rank2, e2, x1)
    return out.reshape(b, s, d)


def kernel(x, mix_norm_g, w_in, q_norm_g, k_norm_g, attn_sink, w_attn_o, ssm_a_re, ssm_a_im, ssm_log_step,
           ssm_b_re, ssm_b_im, ssm_c_re, ssm_c_im, ssm_d, glu_w_a, glu_w_b, w_out, ffn_norm_g, peer_w_query,
           peer_sub_keys_1, peer_sub_keys_2, peer_down, peer_up):
    for l in range(mix_norm_g.shape[0]):
        x = _layer(x, mix_norm_g[l], w_in[l], q_norm_g[l], k_norm_g[l], attn_sink[l], w_attn_o[l],
                   ssm_a_re[l], ssm_a_im[l], ssm_log_step[l], ssm_b_re[l], ssm_b_im[l], ssm_c_re[l],
                   ssm_c_im[l], ssm_d[l], glu_w_a[l], glu_w_b[l], w_out[l], ffn_norm_g[l], peer_w_query[l],
                   peer_sub_keys_1[l], peer_sub_keys_2[l], peer_down[l], peer_up[l])
    return x
```

```python
import functools
import math

import jax
import jax.numpy as jnp
from jax import lax
from jax.experimental import pallas as pl
from jax.experimental.pallas import tpu as pltpu

N_Q_HEADS = 16
N_KV_HEADS = 4
Q_GROUP = N_Q_HEADS // N_KV_HEADS
HEAD_DIM = 128
WINDOW = 128
BLOCK = 128
SSM_GROUP = 16
SSM_STATE = 64
SSM_CHUNK = 16
PEER_HEADS = 8
N_KEYS = 128
PEER_HALF = 128
PEER_TOPK = 16
RMS_EPS = 1e-6
LANES = 128
VMEM_LIMIT = 56 * 1024 * 1024

_F32 = jnp.float32
_BF16 = jnp.bfloat16
_NT = (((1,), (1,)), ((), ()))


def _params(semantics):
    return pltpu.CompilerParams(dimension_semantics=semantics, vmem_limit_bytes=VMEM_LIMIT)


def _gelu(v):
    return 0.5 * v * (1.0 + lax.erf(v * (1.0 / math.sqrt(2.0))))


def _rms_scale(v):
    return lax.rsqrt(jnp.mean(v * v, axis=-1, keepdims=True) + RMS_EPS)


def _inproj_body(x_ref, g_ref, w_ref, o_ref, h_scr):
    @pl.when(pl.program_id(1) == 0)
    def _():
        x = x_ref[...]
        h_scr[...] = (x * _rms_scale(x) * g_ref[...]).astype(h_scr.dtype)

    o_ref[...] = jnp.dot(h_scr[...], w_ref[...], preferred_element_type=_F32).astype(o_ref.dtype)


def _inproj(x2, gain, w_bf16):
    t, d = x2.shape
    n = w_bf16.shape[1]
    tm, tn = min(512, t), 1024
    return pl.pallas_call(
        _inproj_body,
        out_shape=jax.ShapeDtypeStruct((t, n), _BF16),
        grid=(t // tm, n // tn),
        in_specs=[
            pl.BlockSpec((tm, d), lambda i, j: (i, 0)),
            pl.BlockSpec((1, d), lambda i, j: (0, 0)),
            pl.BlockSpec((d, tn), lambda i, j: (0, j)),
        ],
        out_specs=pl.BlockSpec((tm, tn), lambda i, j: (i, j)),
        scratch_shapes=[pltpu.VMEM((tm, d), _BF16)],
        compiler_params=_params(("arbitrary", "arbitrary")),
        name="inproj",
    )(x2, gain.reshape(1, d), w_bf16)


def _attn_body(sink_ref, q_ref, kp_ref, kc_ref, kn_ref, vp_ref, vc_ref, vn_ref, qg_ref, kg_ref, o_ref,
               *, seq_len, slopes):
    i = pl.program_id(1)
    row = lax.broadcasted_iota(jnp.int32, (BLOCK, 3 * BLOCK), 0)
    col = lax.broadcasted_iota(jnp.int32, (BLOCK, 3 * BLOCK), 1)
    dist = jnp.abs(row + BLOCK - col)
    kpos = (i - 1) * BLOCK + col
    valid = (dist <= WINDOW) & (kpos >= 0) & (kpos < seq_len)
    neg_dist = jnp.where(valid, -dist.astype(_F32), -jnp.inf)

    k3 = jnp.concatenate([kp_ref[...], kc_ref[...], kn_ref[...]], axis=0).astype(_F32)
    v3 = jnp.concatenate([vp_ref[...], vc_ref[...], vn_ref[...]], axis=0)
    qf = q_ref[...].astype(_F32)
    q_gain = qg_ref[...] * (HEAD_DIM ** -0.5)
    k_gain = kg_ref[...]

    for kv in range(N_KV_HEADS):
        kh = k3[:, kv * HEAD_DIM:(kv + 1) * HEAD_DIM]
        kh = (kh * _rms_scale(kh) * k_gain).astype(_BF16)
        q_rows = []
        for g in range(Q_GROUP):
            h = kv * Q_GROUP + g
            qh = qf[:, h * HEAD_DIM:(h + 1) * HEAD_DIM]
            q_rows.append((qh * _rms_scale(qh) * q_gain).astype(_BF16))
        scores = lax.dot_general(jnp.concatenate(q_rows, axis=0), kh, _NT,
                                 preferred_element_type=_F32)
        probs, denoms = [], []
        for g in range(Q_GROUP):
            h = kv * Q_GROUP + g
            s = scores[g * BLOCK:(g + 1) * BLOCK] + slopes[h] * neg_dist
            sink = sink_ref[h]
            m = jnp.maximum(jnp.max(s, axis=-1, keepdims=True), sink)
            p = jnp.exp(s - m)
            denoms.append(jnp.sum(p, axis=-1, keepdims=True) + jnp.exp(sink - m))
            probs.append(p.astype(_BF16))
        pv = jnp.dot(jnp.concatenate(probs, axis=0), v3[:, kv * HEAD_DIM:(kv + 1) * HEAD_DIM],
                     preferred_element_type=_F32)
        for g in range(Q_GROUP):
            h = kv * Q_GROUP + g
            o_ref[:, h * HEAD_DIM:(h + 1) * HEAD_DIM] = (
                pv[g * BLOCK:(g + 1) * BLOCK] / denoms[g]).astype(o_ref.dtype)


def _attention(proj3, q_gain, k_gain, sink):
    b, s, _ = proj3.shape
    nb = s // BLOCK
    aw, kw = N_Q_HEADS * HEAD_DIM, N_KV_HEADS * HEAD_DIM
    k_col, v_col = aw // kw, aw // kw + 1
    slopes = tuple(2.0 ** (-8.0 * (h + 1.0) / N_Q_HEADS) for h in range(N_Q_HEADS))

    def kv_spec(col, shift):
        return pl.BlockSpec((None, BLOCK, kw),
                            lambda bi, i: (bi, jnp.clip(i + shift, 0, nb - 1), col))

    return pl.pallas_call(
        functools.partial(_attn_body, seq_len=s, slopes=slopes),
        out_shape=jax.ShapeDtypeStruct((b, s, aw), _BF16),
        grid=(b, nb),
        in_specs=[
            pl.BlockSpec(memory_space=pltpu.SMEM),
            pl.BlockSpec((None, BLOCK, aw), lambda bi, i: (bi, i, 0)),
            kv_spec(k_col, -1), kv_spec(k_col, 0), kv_spec(k_col, 1),
            kv_spec(v_col, -1), kv_spec(v_col, 0), kv_spec(v_col, 1),
            pl.BlockSpec((1, HEAD_DIM), lambda bi, i: (0, 0)),
            pl.BlockSpec((1, HEAD_DIM), lambda bi, i: (0, 0)),
        ],
        out_specs=pl.BlockSpec((None, BLOCK, aw), lambda bi, i: (bi, i, 0)),
        compiler_params=_params(("arbitrary", "arbitrary")),
        name="attn",
    )(sink.astype(_F32), proj3, proj3, proj3, proj3, proj3, proj3, proj3,
      q_gain.reshape(1, HEAD_DIM).astype(_F32), k_gain.reshape(1, HEAD_DIM).astype(_F32))


def _ssm_matrices(a_re, a_im, log_step, b_re, b_im, c_re, c_im, d_skip):
    hp = lax.Precision.HIGHEST
    n_l, n_c, n_s = SSM_CHUNK, SSM_GROUP, SSM_STATE
    n_g = a_re.shape[1]
    a_re, a_im, b_re, b_im, c_re, c_im = (v.astype(_F32) for v in (a_re, a_im, b_re, b_im, c_re, c_im))
    step = jnp.exp(log_step.astype(_F32))[..., None]
    steps = jnp.arange(n_l + 1, dtype=_F32)[:, None, None, None]
    mag = jnp.exp((a_re * step)[None] * steps)
    ang = (a_im * step)[None] * steps
    p_re, p_im = mag * jnp.cos(ang), mag * jnp.sin(ang)
    num_re, num_im = p_re[1] - 1.0, p_im[1]
    den = a_re * a_re + a_im * a_im
    f_re = ((num_re * a_re + num_im * a_im) / den)[..., None]
    f_im = ((num_im * a_re - num_re * a_im) / den)[..., None]
    bb_re, bb_im = f_re * b_re - f_im * b_im, f_re * b_im + f_im * b_re
    pe_re, pe_im = p_re[:, :, :, None, :], p_im[:, :, :, None, :]
    cp_re = c_re[None] * pe_re - c_im[None] * pe_im
    cp_im = c_re[None] * pe_im + c_im[None] * pe_re

    kern = (jnp.einsum('tdgcn,dgne->dtgce', cp_re[:n_l], bb_re, precision=hp)
            - jnp.einsum('tdgcn,dgne->dtgce', cp_im[:n_l], bb_im, precision=hp))
    s_idx = jnp.arange(n_l)[:, None]
    t_idx = jnp.arange(n_l)[None, :]
    tau_f, tau_b = t_idx - s_idx, s_idx - t_idx
    resp = (jnp.where((tau_f >= 0)[:, :, None, None, None], kern[0][jnp.clip(tau_f, 0, n_l - 1)], 0.0)
            + jnp.where((tau_b >= 0)[:, :, None, None, None], kern[1][jnp.clip(tau_b, 0, n_l - 1)], 0.0))
    skip = (jnp.eye(n_l, dtype=_F32)[:, :, None, None, None]
            * (jnp.eye(n_c, dtype=_F32)[None, None, None] * d_skip.astype(_F32).reshape(1, 1, n_g, n_c, 1)))
    m = (resp + skip).transpose(2, 0, 4, 1, 3).reshape(n_g, n_l * n_c, n_l * n_c)

    def pad_lanes(z):
        return jnp.pad(z, [(0, 0)] * (z.ndim - 1) + [(0, LANES - n_s)])

    def state_in(direction, powers):
        q_re, q_im = p_re[powers, direction][:, :, None, :], p_im[powers, direction][:, :, None, :]
        t_re, t_im = bb_re[direction].transpose(0, 2, 1)[None], bb_im[direction].transpose(0, 2, 1)[None]
        return q_re * t_re - q_im * t_im, q_re * t_im + q_im * t_re

    wf_re, wf_im = state_in(0, n_l - 1 - jnp.arange(n_l))
    wb_re, wb_im = state_in(1, jnp.arange(n_l))
    w_in = jnp.concatenate([pad_lanes(wf_re), pad_lanes(wf_im), pad_lanes(wb_re), pad_lanes(wb_im)], axis=-1)
    w_in = w_in.transpose(1, 0, 2, 3).reshape(n_g, n_l * n_c, 4 * LANES)

    def pad_rows(z):
        z = z.transpose(1, 3, 0, 2)
        return jnp.pad(z, [(0, 0), (0, LANES - n_s), (0, 0), (0, 0)]).reshape(n_g, LANES, n_l * n_c)

    out_f, out_b = 1 + jnp.arange(n_l), n_l - jnp.arange(n_l)
    w_out = jnp.concatenate([pad_rows(cp_re[out_f, 0]), pad_rows(-cp_im[out_f, 0]),
                             pad_rows(cp_re[out_b, 1]), pad_rows(-cp_im[out_b, 1])], axis=1)
    lam4 = jnp.stack([pad_lanes(p_re[n_l, 0]), pad_lanes(p_im[n_l, 0]),
                      pad_lanes(p_re[n_l, 1]), pad_lanes(p_im[n_l, 1])], axis=1)
    return m.astype(_BF16), w_in.astype(_BF16), w_out.astype(_BF16), lam4


def _ssm_body(u_ref, m_ref, win_ref, wout_ref, lam_ref, o_ref, inc_scr, st_scr, *, rows_per_chunk, n_chunks):
    nb, nk = rows_per_chunk, n_chunks
    u = u_ref[...]
    inc_scr[...] = jnp.dot(u, win_ref[...], preferred_element_type=_F32)
    lam = lam_ref[...]
    fr_a, fi_a, br_a, bi_a = (jnp.broadcast_to(lam[r:r + 1, :], (nb, LANES)) for r in range(4))

    def step(k, carry):
        f_re, f_im, b_re, b_im = carry
        rf = pl.ds(pl.multiple_of(k * nb, nb), nb)
        rb = pl.ds(pl.multiple_of((nk - 1 - k) * nb, nb), nb)
        st_scr[rf, 0 * LANES:1 * LANES] = f_re
        st_scr[rf, 1 * LANES:2 * LANES] = f_im
        st_scr[rb, 2 * LANES:3 * LANES] = b_re
        st_scr[rb, 3 * LANES:4 * LANES] = b_im
        nf_re = fr_a * f_re - fi_a * f_im + inc_scr[rf, 0 * LANES:1 * LANES]
        nf_im = fr_a * f_im + fi_a * f_re + inc_scr[rf, 1 * LANES:2 * LANES]
        nb_re = br_a * b_re - bi_a * b_im + inc_scr[rb, 2 * LANES:3 * LANES]
        nb_im = br_a * b_im + bi_a * b_re + inc_scr[rb, 3 * LANES:4 * LANES]
        return nf_re, nf_im, nb_re, nb_im

    zero = jnp.zeros((nb, LANES), _F32)
    lax.fori_loop(0, nk, step, (zero, zero, zero, zero), unroll=4)
    y = (jnp.dot(u, m_ref[...], preferred_element_type=_F32)
         + jnp.dot(st_scr[...].astype(_BF16), wout_ref[...], preferred_element_type=_F32))
    o_ref[...] = _gelu(y).astype(o_ref.dtype)


def _ssm(u_chunks, m, w_in, w_out, lam4, rows_per_chunk):
    n_g, rows, width = u_chunks.shape
    return pl.pallas_call(
        functools.partial(_ssm_body, rows_per_chunk=rows_per_chunk, n_chunks=rows // rows_per_chunk),
        out_shape=jax.ShapeDtypeStruct((n_g, rows, width), _BF16),
        grid=(n_g,),
        in_specs=[
            pl.BlockSpec((None, rows, width), lambda g: (g, 0, 0)),
            pl.BlockSpec((None, width, width), lambda g: (g, 0, 0)),
            pl.BlockSpec((None, width, 4 * LANES), lambda g: (g, 0, 0)),
            pl.BlockSpec((None, 4 * LANES, width), lambda g: (g, 0, 0)),
            pl.BlockSpec((None, 4, LANES), lambda g: (g, 0, 0)),
        ],
        out_specs=pl.BlockSpec((None, rows, width), lambda g: (g, 0, 0)),
        scratch_shapes=[pltpu.VMEM((rows, 4 * LANES), _F32), pltpu.VMEM((rows, 4 * LANES), _F32)],
        compiler_params=_params(("arbitrary",)),
        name="ssm",
    )(u_chunks, m, w_in, w_out, lam4)


def _merge_body(attn_ref, yact_ref, ga_ref, gb_ref, wo_ref, wa_ref, wb_ref, o_ref):
    y_a = jnp.dot(attn_ref[...], wo_ref[...], preferred_element_type=_F32)
    yact = yact_ref[...]
    glu = (jnp.dot(yact, wa_ref[...], preferred_element_type=_F32)
           * jax.nn.sigmoid(jnp.dot(yact, wb_ref[...], preferred_element_type=_F32)))
    o_ref[...] = (jax.nn.sigmoid(ga_ref[...].astype(_F32)) * y_a
                  + jax.nn.sigmoid(gb_ref[...].astype(_F32)) * glu).astype(o_ref.dtype)


def _merge(attn2, yact2, proj, gate_a_col, gate_b_col, wo, wa, wb):
    t, aw = attn2.shape
    sw = yact2.shape[1]
    d = wo.shape[1]
    tm, tn = min(512, t), 1024
    ga0, gb0 = gate_a_col // tn, gate_b_col // tn
    return pl.pallas_call(
        _merge_body,
        out_shape=jax.ShapeDtypeStruct((t, d), _BF16),
        grid=(t // tm, d // tn),
        in_specs=[
            pl.BlockSpec((tm, aw), lambda i, j: (i, 0)),
            pl.BlockSpec((tm, sw), lambda i, j: (i, 0)),
            pl.BlockSpec((tm, tn), lambda i, j: (i, ga0 + j)),
            pl.BlockSpec((tm, tn), lambda i, j: (i, gb0 + j)),
            pl.BlockSpec((aw, tn), lambda i, j: (0, j)),
            pl.BlockSpec((sw, tn), lambda i, j: (0, j)),
            pl.BlockSpec((sw, tn), lambda i, j: (0, j)),
        ],
        out_specs=pl.BlockSpec((tm, tn), lambda i, j: (i, j)),
        compiler_params=_params(("arbitrary", "arbitrary")),
        name="merge",
    )(attn2, yact2, proj, proj, wo, wa, wb)


def _outproj_body(x_ref, m_ref, w_ref, g_ref, x1_ref, h2_ref):
    x1 = x_ref[...] + jnp.dot(m_ref[...], w_ref[...], preferred_element_type=_F32)
    x1_ref[...] = x1
    h2_ref[...] = (x1 * _rms_scale(x1) * g_ref[...]).astype(h2_ref.dtype)


def _outproj(x2, merged, w_out, gain):
    t, d = x2.shape
    tm = min(512, t)
    return pl.pallas_call(
        _outproj_body,
        out_shape=(jax.ShapeDtypeStruct((t, d), _F32), jax.ShapeDtypeStruct((t, d), _BF16)),
        grid=(t // tm,),
        in_specs=[
            pl.BlockSpec((tm, d), lambda i: (i, 0)),
            pl.BlockSpec((tm, d), lambda i: (i, 0)),
            pl.BlockSpec((d, d), lambda i: (0, 0)),
            pl.BlockSpec((1, d), lambda i: (0, 0)),
        ],
        out_specs=(pl.BlockSpec((tm, d), lambda i: (i, 0)), pl.BlockSpec((tm, d), lambda i: (i, 0))),
        compiler_params=_params(("arbitrary",)),
        name="outproj",
    )(x2, merged, w_out, gain.reshape(1, d))


_CAND_WIDE = 4
_CAND_TAIL = tuple((p1, p2) for p1 in range(_CAND_WIDE, PEER_TOPK) for p2 in range(PEER_TOPK)
                   if (p1 + 1) * (p2 + 1) <= PEER_TOPK)
_SUBLANES = 8


def _top_rows(vals, k, exact):
    n = vals.shape[0]
    row = lax.broadcasted_iota(jnp.int32, vals.shape, 0).astype(_F32)
    rank = jnp.full(vals.shape, float(k), _F32)
    work = vals
    tops = []
    for r in range(k):
        best = jnp.max(work, axis=0, keepdims=True)
        hit = work == best
        if exact:
            first = jnp.min(jnp.where(hit, row, float(n)), axis=0, keepdims=True)
            hit = row == first
        rank = jnp.where(hit, float(r), rank)
        work = jnp.where(hit, -jnp.inf, work)
        tops.append(best)
    return tops, rank


def _selection(s1, s2, exact):
    k = PEER_TOPK
    t1, rank1 = _top_rows(s1, k, exact)
    t2, rank2 = _top_rows(s2, k, exact)
    v2 = jnp.concatenate(t2, axis=0)
    n_tail = len(_CAND_TAIL)
    pad = [jnp.full_like(t1[0], -jnp.inf)] * (-n_tail % _SUBLANES)
    cand = jnp.concatenate([t1[p] + v2 for p in range(_CAND_WIDE)]
                           + [t1[p1] + t2[p2] for p1, p2 in _CAND_TAIL] + pad, axis=0)
    tc, rank_c = _top_rows(cand, k, exact)
    chosen = jnp.where(rank_c < k, 1.0, 0.0)
    per_rank = [jnp.sum(chosen[p * k:(p + 1) * k], axis=0, keepdims=True) for p in range(_CAND_WIDE)]
    for p1 in range(_CAND_WIDE, k):
        rows = [_CAND_WIDE * k + i for i, (a, _) in enumerate(_CAND_TAIL) if a == p1]
        per_rank.append(jnp.sum(chosen[rows[0]:rows[-1] + 1], axis=0, keepdims=True))
    count = jnp.zeros(s1.shape, _F32)
    for p in range(k):
        count = jnp.where(rank1 == p, per_rank[p], count)
    top = jnp.concatenate(tc, axis=0)
    z = jnp.sum(jnp.exp(top - tc[0]), axis=0, keepdims=True)
    coef = jnp.exp(s1 - t1[0]) / z
    e2 = jnp.exp(s2 - t2[0])

    def excess(rk):
        return jnp.abs(jnp.sum(jnp.where(rk < k, 1.0, 0.0), axis=0, keepdims=True) - k)

    bad = jnp.max(excess(rank1) + excess(rank2) + excess(rank_c))
    return count, coef, rank2, e2, bad


def _select_body(h2_ref, wq_ref, k1_ref, k2_ref, cnt_ref, coef_ref, rank_ref, e2_ref):
    q_t = lax.dot_general(wq_ref[...], h2_ref[...], _NT, preferred_element_type=_F32)
    s1 = jnp.dot(k1_ref[...], q_t[:PEER_HALF].astype(_BF16), preferred_element_type=_F32)
    s2 = jnp.dot(k2_ref[...], q_t[PEER_HALF:].astype(_BF16), preferred_element_type=_F32)

    def emit(count, coef, rank2, e2):
        cnt_ref[...] = count
        coef_ref[...] = coef
        rank_ref[...] = rank2.astype(rank_ref.dtype)
        e2_ref[...] = e2.astype(e2_ref.dtype)

    *fast, bad = _selection(s1, s2, exact=False)
    emit(*fast)

    @pl.when(bad > 0.0)
    def _():
        emit(*_selection(s1, s2, exact=True)[:4])


def _select(h2, wq_t, keys1, keys2):
    t, d = h2.shape
    tm = min(512, t)
    qd = 2 * PEER_HALF
    out = jax.ShapeDtypeStruct((PEER_HEADS, N_KEYS, t), _F32)
    out16 = jax.ShapeDtypeStruct((PEER_HEADS, N_KEYS, t), _BF16)
    spec = pl.BlockSpec((None, N_KEYS, tm), lambda i, h: (h, 0, i))
    return pl.pallas_call(
        _select_body,
        out_shape=(out, out, out16, out16),
        grid=(t // tm, PEER_HEADS),
        in_specs=[
            pl.BlockSpec((tm, d), lambda i, h: (i, 0)),
            pl.BlockSpec((qd, d), lambda i, h: (h, 0)),
            pl.BlockSpec((None, N_KEYS, PEER_HALF), lambda i, h: (h, 0, 0)),
            pl.BlockSpec((None, N_KEYS, PEER_HALF), lambda i, h: (h, 0, 0)),
        ],
        out_specs=(spec, spec, spec, spec),
        compiler_params=_params(("arbitrary", "arbitrary")),
        name="select",
    )(h2, wq_t, keys1, keys2)


def _peer_step(n, h2_ref, down_ref, upt_ref, cnt_ref, coef_ref, rank_ref, e2_ref, x1_ref, o_ref, acc_scr,
               gact_out, gact_in, wg_out, wg_in, *, key_rows, tiles):
    drained = n - 2

    @pl.when(jnp.logical_or(n < 2, lax.rem(drained, tiles) == 0))
    def _():
        acc_scr[...] = jnp.zeros_like(acc_scr)

    act = lax.dot_general(down_ref[...], h2_ref[...], _NT, preferred_element_type=_F32)
    gact_out[...] = _gelu(act).astype(_BF16)

    block = (N_KEYS, gact_in.shape[1])
    zero = jnp.zeros(block, _BF16)
    for a in range(key_rows):
        rows = slice(a * N_KEYS, (a + 1) * N_KEYS)
        w = None
        for h in range(PEER_HEADS):
            cnt = jnp.broadcast_to(cnt_ref[h, a:a + 1, :].astype(_BF16), block)
            coef = jnp.broadcast_to(coef_ref[h, a:a + 1, :].astype(_BF16), block)
            term = jnp.where(rank_ref[h] < cnt, e2_ref[h], zero) * coef
            w = term if w is None else w + term
        wg_out[rows, :] = w * gact_in[rows, :]

    acc_scr[...] += jnp.dot(upt_ref[...], wg_in[...], preferred_element_type=_F32)

    @pl.when(jnp.logical_and(n >= 2, lax.rem(drained, tiles) == tiles - 1))
    def _():
        o_ref[...] = x1_ref[...] + acc_scr[...].T


def _peer_body(h2_ref, down_ref, upt_ref, cnt_ref, coef_ref, rank_ref, e2_ref, x1_ref, o_ref,
               acc_scr, gact0, gact1, wg0, wg1, *, key_rows, tiles):
    n = pl.program_id(0)
    io = (h2_ref, down_ref, upt_ref, cnt_ref, coef_ref, rank_ref, e2_ref, x1_ref, o_ref, acc_scr)
    step = functools.partial(_peer_step, n, *io, key_rows=key_rows, tiles=tiles)

    @pl.when(n == 0)
    def _():
        gact1[...] = jnp.zeros_like(gact1)
        wg0[...] = jnp.zeros_like(wg0)

    @pl.when(lax.rem(n, 2) == 0)
    def _():
        step(gact_out=gact0, gact_in=gact1, wg_out=wg1, wg_in=wg0)

    @pl.when(lax.rem(n, 2) == 1)
    def _():
        step(gact_out=gact1, gact_in=gact0, wg_out=wg0, wg_in=wg1)


def _peer(h2, down, up_t, count, coef, rank2, e2, x1):
    t, d = h2.shape
    n_e = down.shape[0]
    tm, te = min(512, t), 1024
    key_rows = te // N_KEYS
    tiles = n_e // te
    total = (t // tm) * tiles
    once = pl.Buffered(1)

    def tile_of(stage_lag):
        def index(n):
            m = jnp.clip(n - stage_lag, 0, total - 1)
            return m // tiles, lax.rem(m, tiles)
        return index

    act_tile, weight_tile, drain_tile = tile_of(0), tile_of(1), tile_of(2)
    buf = pltpu.VMEM((te, tm), _BF16)
    return pl.pallas_call(
        functools.partial(_peer_body, key_rows=key_rows, tiles=tiles),
        out_shape=jax.ShapeDtypeStruct((t, d), _F32),
        grid=(total + 2,),
        in_specs=[
            pl.BlockSpec((tm, d), lambda n: (act_tile(n)[0], 0), pipeline_mode=once),
            pl.BlockSpec((te, d), lambda n: (act_tile(n)[1], 0)),
            pl.BlockSpec((d, te), lambda n: (0, drain_tile(n)[1])),
            pl.BlockSpec((PEER_HEADS, key_rows, tm), lambda n: (0, weight_tile(n)[1], weight_tile(n)[0])),
            pl.BlockSpec((PEER_HEADS, key_rows, tm), lambda n: (0, weight_tile(n)[1], weight_tile(n)[0])),
            pl.BlockSpec((PEER_HEADS, N_KEYS, tm), lambda n: (0, 0, weight_tile(n)[0]), pipeline_mode=once),
            pl.BlockSpec((PEER_HEADS, N_KEYS, tm), lambda n: (0, 0, weight_tile(n)[0]), pipeline_mode=once),
            pl.BlockSpec((tm, d), lambda n: (drain_tile(n)[0], 0), pipeline_mode=once),
        ],
        out_specs=pl.BlockSpec((tm, d), lambda n: (drain_tile(n)[0], 0)),
        scratch_shapes=[pltpu.VMEM((d, tm), _F32), buf, buf, buf, buf],
        compiler_params=_params(("arbitrary",)),
        name="peer",
    )(h2, down, up_t, count, coef, rank2, e2, x1)


def _layer(x, mix_g, w_in, q_g, k_g, sink, w_attn_o, a_re, a_im, log_step, b_re, b_im, c_re, c_im, d_skip,
           glu_a, glu_b, w_out, ffn_g, wq, keys1, keys2, down, up):
    b, s, d = x.shape
    t = b * s
    aw, kw = N_Q_HEADS * HEAD_DIM, N_KV_HEADS * HEAD_DIM
    sw = d_skip.shape[0]
    n_g = sw // SSM_GROUP
    u_col = aw + 2 * kw
    ga_col = u_col + sw
    gb_col = ga_col + d
    x2 = x.reshape(t, d)

    proj = _inproj(x2, mix_g, w_in.astype(_BF16))
    attn = _attention(proj.reshape(b, s, -1), q_g, k_g, sink).reshape(t, aw)

    n_k = s // SSM_CHUNK
    u = proj[:, u_col:u_col + sw].reshape(b, n_k, SSM_CHUNK, n_g, SSM_GROUP)
    u_chunks = u.transpose(3, 1, 0, 2, 4).reshape(n_g, n_k * b, SSM_CHUNK * SSM_GROUP)
    m, s_in, s_out, lam4 = _ssm_matrices(a_re, a_im, log_step, b_re, b_im, c_re, c_im, d_skip)
    yact = _ssm(u_chunks, m, s_in, s_out, lam4, b)
    yact = yact.reshape(n_g, n_k, b, SSM_CHUNK, SSM_GROUP).transpose(2, 1, 3, 0, 4).reshape(t, sw)

    merged = _merge(attn, yact, proj, ga_col, gb_col,
                    w_attn_o.astype(_BF16), glu_a.astype(_BF16), glu_b.astype(_BF16))
    x1, h2 = _outproj(x2, merged, w_out.astype(_BF16), ffn_g)

    count, coef, rank2, e2 = _select(h2, wq.T.astype(_BF16), keys1.astype(_BF16), keys2.astype(_BF16))
    out = _peer(h2, down.astype(_BF16), up.T.astype(_BF16), count, coef, rank2, e2, x1)
    return out.reshape(b, s, d)


def kernel(x, mix_norm_g, w_in, q_norm_g, k_norm_g, attn_sink, w_attn_o, ssm_a_re, ssm_a_im, ssm_log_step,
           ssm_b_re, ssm_b_im, ssm_c_re, ssm_c_im, ssm_d, glu_w_a, glu_w_b, w_out, ffn_norm_g, peer_w_query,
           peer_sub_keys_1, peer_sub_keys_2, peer_down, peer_up):
    for l in range(mix_norm_g.shape[0]):
        x = _layer(x, mix_norm_g[l], w_in[l], q_norm_g[l], k_norm_g[l], attn_sink[l], w_attn_o[l],
                   ssm_a_re[l], ssm_a_im[l], ssm_log_step[l], ssm_b_re[l], ssm_b_im[l], ssm_c_re[l],
                   ssm_c_im[l], ssm_d[l], glu_w_a[l], glu_w_b[l], w_out[l], ffn_norm_g[l], peer_w_query[l],
                   peer_sub_keys_1[l], peer_sub_keys_2[l], peer_down[l], peer_up[l])
    return x
```

```python
import functools
import math

import jax
import jax.numpy as jnp
from jax import lax
from jax.experimental import pallas as pl
from jax.experimental.pallas import tpu as pltpu

N_Q_HEADS = 16
N_KV_HEADS = 4
Q_GROUP = N_Q_HEADS // N_KV_HEADS
HEAD_DIM = 128
WINDOW = 128
BLOCK = 128
SSM_GROUP = 16
SSM_STATE = 64
SSM_CHUNK = 16
PEER_HEADS = 8
N_KEYS = 128
PEER_HALF = 128
PEER_TOPK = 16
RMS_EPS = 1e-6
LANES = 128
VMEM_LIMIT = 56 * 1024 * 1024

_F32 = jnp.float32
_BF16 = jnp.bfloat16
_NT = (((1,), (1,)), ((), ()))


def _params(semantics):
    return pltpu.CompilerParams(dimension_semantics=semantics, vmem_limit_bytes=VMEM_LIMIT)


def _gelu(v):
    return 0.5 * v * (1.0 + lax.erf(v * (1.0 / math.sqrt(2.0))))


def _rms_scale(v):
    return lax.rsqrt(jnp.mean(v * v, axis=-1, keepdims=True) + RMS_EPS)


def _inproj_body(x_ref, g_ref, w_ref, o_ref, h_scr):
    @pl.when(pl.program_id(1) == 0)
    def _():
        x = x_ref[...]
        h_scr[...] = (x * _rms_scale(x) * g_ref[...]).astype(h_scr.dtype)

    o_ref[...] = jnp.dot(h_scr[...], w_ref[...], preferred_element_type=_F32).astype(o_ref.dtype)


def _inproj(x2, gain, w_bf16):
    t, d = x2.shape
    n = w_bf16.shape[1]
    tm, tn = min(1024, t), 1024
    return pl.pallas_call(
        _inproj_body,
        out_shape=jax.ShapeDtypeStruct((t, n), _BF16),
        grid=(t // tm, n // tn),
        in_specs=[
            pl.BlockSpec((tm, d), lambda i, j: (i, 0)),
            pl.BlockSpec((1, d), lambda i, j: (0, 0)),
            pl.BlockSpec((d, tn), lambda i, j: (0, j)),
        ],
        out_specs=pl.BlockSpec((tm, tn), lambda i, j: (i, j)),
        scratch_shapes=[pltpu.VMEM((tm, d), _BF16)],
        compiler_params=_params(("arbitrary", "arbitrary")),
        name="inproj",
    )(x2, gain.reshape(1, d), w_bf16)


def _attn_body(sink_ref, q_ref, kp_ref, kc_ref, kn_ref, vp_ref, vc_ref, vn_ref, qg_ref, kg_ref, o_ref,
               *, seq_len, slopes):
    i = pl.program_id(1)
    row = lax.broadcasted_iota(jnp.int32, (BLOCK, 3 * BLOCK), 0)
    col = lax.broadcasted_iota(jnp.int32, (BLOCK, 3 * BLOCK), 1)
    dist = jnp.abs(row + BLOCK - col)
    kpos = (i - 1) * BLOCK + col
    valid = (dist <= WINDOW) & (kpos >= 0) & (kpos < seq_len)
    neg_dist = jnp.where(valid, -dist.astype(_F32), -jnp.inf)

    k3 = jnp.concatenate([kp_ref[...], kc_ref[...], kn_ref[...]], axis=0).astype(_F32)
    v3 = jnp.concatenate([vp_ref[...], vc_ref[...], vn_ref[...]], axis=0)
    qf = q_ref[...].astype(_F32)
    q_gain = qg_ref[...] * (HEAD_DIM ** -0.5)
    k_gain = kg_ref[...]

    for kv in range(N_KV_HEADS):
        kh = k3[:, kv * HEAD_DIM:(kv + 1) * HEAD_DIM]
        kh = (kh * _rms_scale(kh) * k_gain).astype(_BF16)
        q_rows = []
        for g in range(Q_GROUP):
            h = kv * Q_GROUP + g
            qh = qf[:, h * HEAD_DIM:(h + 1) * HEAD_DIM]
            q_rows.append((qh * _rms_scale(qh) * q_gain).astype(_BF16))
        scores = lax.dot_general(jnp.concatenate(q_rows, axis=0), kh, _NT,
                                 preferred_element_type=_F32)
        probs, denoms = [], []
        for g in range(Q_GROUP):
            h = kv * Q_GROUP + g
            s = scores[g * BLOCK:(g + 1) * BLOCK] + slopes[h] * neg_dist
            sink = sink_ref[h]
            m = jnp.maximum(jnp.max(s, axis=-1, keepdims=True), sink)
            p = jnp.exp(s - m)
            denoms.append(jnp.sum(p, axis=-1, keepdims=True) + jnp.exp(sink - m))
            probs.append(p.astype(_BF16))
        pv = jnp.dot(jnp.concatenate(probs, axis=0), v3[:, kv * HEAD_DIM:(kv + 1) * HEAD_DIM],
                     preferred_element_type=_F32)
        for g in range(Q_GROUP):
            h = kv * Q_GROUP + g
            o_ref[:, h * HEAD_DIM:(h + 1) * HEAD_DIM] = (
                pv[g * BLOCK:(g + 1) * BLOCK] / denoms[g]).astype(o_ref.dtype)


def _attention(proj3, q_gain, k_gain, sink):
    b, s, _ = proj3.shape
    nb = s // BLOCK
    aw, kw = N_Q_HEADS * HEAD_DIM, N_KV_HEADS * HEAD_DIM
    k_col, v_col = aw // kw, aw // kw + 1
    slopes = tuple(2.0 ** (-8.0 * (h + 1.0) / N_Q_HEADS) for h in range(N_Q_HEADS))

    def kv_spec(col, shift):
        return pl.BlockSpec((None, BLOCK, kw),
                            lambda bi, i: (bi, jnp.clip(i + shift, 0, nb - 1), col))

    return pl.pallas_call(
        functools.partial(_attn_body, seq_len=s, slopes=slopes),
        out_shape=jax.ShapeDtypeStruct((b, s, aw), _BF16),
        grid=(b, nb),
        in_specs=[
            pl.BlockSpec(memory_space=pltpu.SMEM),
            pl.BlockSpec((None, BLOCK, aw), lambda bi, i: (bi, i, 0)),
            kv_spec(k_col, -1), kv_spec(k_col, 0), kv_spec(k_col, 1),
            kv_spec(v_col, -1), kv_spec(v_col, 0), kv_spec(v_col, 1),
            pl.BlockSpec((1, HEAD_DIM), lambda bi, i: (0, 0)),
            pl.BlockSpec((1, HEAD_DIM), lambda bi, i: (0, 0)),
        ],
        out_specs=pl.BlockSpec((None, BLOCK, aw), lambda bi, i: (bi, i, 0)),
        compiler_params=_params(("arbitrary", "arbitrary")),
        name="attn",
    )(sink.astype(_F32), proj3, proj3, proj3, proj3, proj3, proj3, proj3,
      q_gain.reshape(1, HEAD_DIM).astype(_F32), k_gain.reshape(1, HEAD_DIM).astype(_F32))


def _ssm_matrices(a_re, a_im, log_step, b_re, b_im, c_re, c_im, d_skip):
    hp = lax.Precision.HIGHEST
    n_l, n_c, n_s = SSM_CHUNK, SSM_GROUP, SSM_STATE
    n_g = a_re.shape[1]
    a_re, a_im, b_re, b_im, c_re, c_im = (v.astype(_F32) for v in (a_re, a_im, b_re, b_im, c_re, c_im))
    step = jnp.exp(log_step.astype(_F32))[..., None]
    steps = jnp.arange(n_l + 1, dtype=_F32)[:, None, None, None]
    mag = jnp.exp((a_re * step)[None] * steps)
    ang = (a_im * step)[None] * steps
    p_re, p_im = mag * jnp.cos(ang), mag * jnp.sin(ang)
    num_re, num_im = p_re[1] - 1.0, p_im[1]
    den = a_re * a_re + a_im * a_im
    f_re = ((num_re * a_re + num_im * a_im) / den)[..., None]
    f_im = ((num_im * a_re - num_re * a_im) / den)[..., None]
    bb_re, bb_im = f_re * b_re - f_im * b_im, f_re * b_im + f_im * b_re
    pe_re, pe_im = p_re[:, :, :, None, :], p_im[:, :, :, None, :]
    cp_re = c_re[None] * pe_re - c_im[None] * pe_im
    cp_im = c_re[None] * pe_im + c_im[None] * pe_re

    kern = (jnp.einsum('tdgcn,dgne->dtgce', cp_re[:n_l], bb_re, precision=hp)
            - jnp.einsum('tdgcn,dgne->dtgce', cp_im[:n_l], bb_im, precision=hp))
    s_idx = jnp.arange(n_l)[:, None]
    t_idx = jnp.arange(n_l)[None, :]
    tau_f, tau_b = t_idx - s_idx, s_idx - t_idx
    resp = (jnp.where((tau_f >= 0)[:, :, None, None, None], kern[0][jnp.clip(tau_f, 0, n_l - 1)], 0.0)
            + jnp.where((tau_b >= 0)[:, :, None, None, None], kern[1][jnp.clip(tau_b, 0, n_l - 1)], 0.0))
    skip = (jnp.eye(n_l, dtype=_F32)[:, :, None, None, None]
            * (jnp.eye(n_c, dtype=_F32)[None, None, None] * d_skip.astype(_F32).reshape(1, 1, n_g, n_c, 1)))
    m = (resp + skip).transpose(2, 0, 4, 1, 3).reshape(n_g, n_l * n_c, n_l * n_c)

    def pad_lanes(z):
        return jnp.pad(z, [(0, 0)] * (z.ndim - 1) + [(0, LANES - n_s)])

    def state_in(direction, powers):
        q_re, q_im = p_re[powers, direction][:, :, None, :], p_im[powers, direction][:, :, None, :]
        t_re, t_im = bb_re[direction].transpose(0, 2, 1)[None], bb_im[direction].transpose(0, 2, 1)[None]
        return q_re * t_re - q_im * t_im, q_re * t_im + q_im * t_re

    wf_re, wf_im = state_in(0, n_l - 1 - jnp.arange(n_l))
    wb_re, wb_im = state_in(1, jnp.arange(n_l))
    w_in = jnp.concatenate([pad_lanes(wf_re), pad_lanes(wf_im), pad_lanes(wb_re), pad_lanes(wb_im)], axis=-1)
    w_in = w_in.transpose(1, 0, 2, 3).reshape(n_g, n_l * n_c, 4 * LANES)

    def pad_rows(z):
        z = z.transpose(1, 3, 0, 2)
        return jnp.pad(z, [(0, 0), (0, LANES - n_s), (0, 0), (0, 0)]).reshape(n_g, LANES, n_l * n_c)

    out_f, out_b = 1 + jnp.arange(n_l), n_l - jnp.arange(n_l)
    w_out = jnp.concatenate([pad_rows(cp_re[out_f, 0]), pad_rows(-cp_im[out_f, 0]),
                             pad_rows(cp_re[out_b, 1]), pad_rows(-cp_im[out_b, 1])], axis=1)
    lam4 = jnp.stack([pad_lanes(p_re[n_l, 0]), pad_lanes(p_im[n_l, 0]),
                      pad_lanes(p_re[n_l, 1]), pad_lanes(p_im[n_l, 1])], axis=1)
    return m.astype(_BF16), w_in.astype(_BF16), w_out.astype(_BF16), lam4


def _ssm_body(u_ref, m_ref, win_ref, wout_ref, lam_ref, o_ref, inc_scr, st_scr, *, rows_per_chunk, n_chunks):
    nb, nk = rows_per_chunk, n_chunks
    u = u_ref[...]
    inc_scr[...] = jnp.dot(u, win_ref[...], preferred_element_type=_F32)
    lam = lam_ref[...]
    fr_a, fi_a, br_a, bi_a = (jnp.broadcast_to(lam[r:r + 1, :], (nb, LANES)) for r in range(4))

    def step(k, carry):
        f_re, f_im, b_re, b_im = carry
        rf = pl.ds(pl.multiple_of(k * nb, nb), nb)
        rb = pl.ds(pl.multiple_of((nk - 1 - k) * nb, nb), nb)
        st_scr[rf, 0 * LANES:1 * LANES] = f_re
        st_scr[rf, 1 * LANES:2 * LANES] = f_im
        st_scr[rb, 2 * LANES:3 * LANES] = b_re
        st_scr[rb, 3 * LANES:4 * LANES] = b_im
        nf_re = fr_a * f_re - fi_a * f_im + inc_scr[rf, 0 * LANES:1 * LANES]
        nf_im = fr_a * f_im + fi_a * f_re + inc_scr[rf, 1 * LANES:2 * LANES]
        nb_re = br_a * b_re - bi_a * b_im + inc_scr[rb, 2 * LANES:3 * LANES]
        nb_im = br_a * b_im + bi_a * b_re + inc_scr[rb, 3 * LANES:4 * LANES]
        return nf_re, nf_im, nb_re, nb_im

    zero = jnp.zeros((nb, LANES), _F32)
    lax.fori_loop(0, nk, step, (zero, zero, zero, zero), unroll=4)
    y = (jnp.dot(u, m_ref[...], preferred_element_type=_F32)
         + jnp.dot(st_scr[...].astype(_BF16), wout_ref[...], preferred_element_type=_F32))
    o_ref[...] = _gelu(y).astype(o_ref.dtype)


def _ssm(u_chunks, m, w_in, w_out, lam4, rows_per_chunk):
    n_g, rows, width = u_chunks.shape
    return pl.pallas_call(
        functools.partial(_ssm_body, rows_per_chunk=rows_per_chunk, n_chunks=rows // rows_per_chunk),
        out_shape=jax.ShapeDtypeStruct((n_g, rows, width), _BF16),
        grid=(n_g,),
        in_specs=[
            pl.BlockSpec((None, rows, width), lambda g: (g, 0, 0)),
            pl.BlockSpec((None, width, width), lambda g: (g, 0, 0)),
            pl.BlockSpec((None, width, 4 * LANES), lambda g: (g, 0, 0)),
            pl.BlockSpec((None, 4 * LANES, width), lambda g: (g, 0, 0)),
            pl.BlockSpec((None, 4, LANES), lambda g: (g, 0, 0)),
        ],
        out_specs=pl.BlockSpec((None, rows, width), lambda g: (g, 0, 0)),
        scratch_shapes=[pltpu.VMEM((rows, 4 * LANES), _F32), pltpu.VMEM((rows, 4 * LANES), _F32)],
        compiler_params=_params(("arbitrary",)),
        name="ssm",
    )(u_chunks, m, w_in, w_out, lam4)


def _merge_body(attn_ref, yact_ref, ga_ref, gb_ref, wo_ref, wa_ref, wb_ref, o_ref):
    y_a = jnp.dot(attn_ref[...], wo_ref[...], preferred_element_type=_F32)
    yact = yact_ref[...]
    glu = (jnp.dot(yact, wa_ref[...], preferred_element_type=_F32)
           * jax.nn.sigmoid(jnp.dot(yact, wb_ref[...], preferred_element_type=_F32)))
    o_ref[...] = (jax.nn.sigmoid(ga_ref[...].astype(_F32)) * y_a
                  + jax.nn.sigmoid(gb_ref[...].astype(_F32)) * glu).astype(o_ref.dtype)


def _merge(attn2, yact2, proj, gate_a_col, gate_b_col, wo, wa, wb):
    t, aw = attn2.shape
    sw = yact2.shape[1]
    d = wo.shape[1]
    tm, tn = min(512, t), d
    ga0, gb0 = gate_a_col // tn, gate_b_col // tn
    once = pl.Buffered(1)
    return pl.pallas_call(
        _merge_body,
        out_shape=jax.ShapeDtypeStruct((t, d), _BF16),
        grid=(t // tm, d // tn),
        in_specs=[
            pl.BlockSpec((tm, aw), lambda i, j: (i, 0)),
            pl.BlockSpec((tm, sw), lambda i, j: (i, 0)),
            pl.BlockSpec((tm, tn), lambda i, j: (i, ga0 + j)),
            pl.BlockSpec((tm, tn), lambda i, j: (i, gb0 + j)),
            pl.BlockSpec((aw, tn), lambda i, j: (0, j), pipeline_mode=once),
            pl.BlockSpec((sw, tn), lambda i, j: (0, j), pipeline_mode=once),
            pl.BlockSpec((sw, tn), lambda i, j: (0, j), pipeline_mode=once),
        ],
        out_specs=pl.BlockSpec((tm, tn), lambda i, j: (i, j)),
        compiler_params=_params(("arbitrary", "arbitrary")),
        name="merge",
    )(attn2, yact2, proj, proj, wo, wa, wb)


def _outproj_body(x_ref, m_ref, w_ref, g_ref, wq_ref, x1_ref, h2_ref, q_ref):
    x1 = x_ref[...] + jnp.dot(m_ref[...], w_ref[...], preferred_element_type=_F32)
    x1_ref[...] = x1
    h2 = (x1 * _rms_scale(x1) * g_ref[...]).astype(h2_ref.dtype)
    h2_ref[...] = h2
    q_ref[...] = jnp.dot(h2, wq_ref[...], preferred_element_type=_F32).astype(q_ref.dtype)


def _outproj(x2, merged, w_out, gain, wq):
    t, d = x2.shape
    qw = wq.shape[1]
    tm = min(512, t)
    rows = lambda width: pl.BlockSpec((tm, width), lambda i: (i, 0))
    whole = lambda arr: pl.BlockSpec(arr.shape, lambda i: (0, 0))
    return pl.pallas_call(
        _outproj_body,
        out_shape=(jax.ShapeDtypeStruct((t, d), _F32), jax.ShapeDtypeStruct((t, d), _BF16),
                   jax.ShapeDtypeStruct((t, qw), _BF16)),
        grid=(t // tm,),
        in_specs=[rows(d), rows(d), whole(w_out), pl.BlockSpec((1, d), lambda i: (0, 0)), whole(wq)],
        out_specs=(rows(d), rows(d), rows(qw)),
        compiler_params=_params(("arbitrary",)),
        name="outproj",
    )(x2, merged, w_out, gain.reshape(1, d), wq)


_CAND_WIDE = 4
_CAND_TAIL = tuple((p1, p2) for p1 in range(_CAND_WIDE, PEER_TOPK) for p2 in range(PEER_TOPK)
                   if (p1 + 1) * (p2 + 1) <= PEER_TOPK)
_SUBLANES = 8


def _top_rows(vals, k, exact, want_rank):
    n = vals.shape[0]
    row = lax.broadcasted_iota(jnp.int32, vals.shape, 0).astype(_F32) if exact else None
    rank = jnp.full(vals.shape, float(k), _F32) if want_rank else None
    work = vals
    tops = []
    for r in range(k):
        best = jnp.max(work, axis=0, keepdims=True)
        hit = work == best
        if exact:
            first = jnp.min(jnp.where(hit, row, float(n)), axis=0, keepdims=True)
            hit = row == first
        if want_rank:
            rank = jnp.where(hit, float(r), rank)
        work = jnp.where(hit, -jnp.inf, work)
        tops.append(best)
    return jnp.concatenate(tops, axis=0), rank, work


def _extracted(left):
    return jnp.sum(jnp.where(left == -jnp.inf, 1.0, 0.0), axis=0, keepdims=True)


def _rank_keys(s1, s2, exact):
    k = PEER_TOPK
    top1, rank1, left1 = _top_rows(s1, k, exact, want_rank=exact)
    top2, rank2, left2 = _top_rows(s2, k, exact, want_rank=True)
    e2 = jnp.exp(s2 - top2[0:1])
    bad = None if exact else jnp.abs(_extracted(left1) - k) + jnp.abs(_extracted(left2) - k)
    return top1, top2, rank1, rank2, e2, bad


def _rank_pairs(s1, top1, top2, rank1, exact):
    k = PEER_TOPK
    n_pad = -len(_CAND_TAIL) % _SUBLANES
    pad = [jnp.full_like(top1[0:1], -jnp.inf)] * n_pad
    cand = jnp.concatenate([top1[p:p + 1] + top2 for p in range(_CAND_WIDE)]
                           + [top1[p1:p1 + 1] + top2[p2:p2 + 1] for p1, p2 in _CAND_TAIL] + pad, axis=0)
    top_c, _, left_c = _top_rows(cand, k, exact, want_rank=False)
    chosen = jnp.where(left_c == -jnp.inf, 1.0, 0.0)
    per_rank = [jnp.sum(chosen[p * k:(p + 1) * k], axis=0, keepdims=True) for p in range(_CAND_WIDE)]
    for p1 in range(_CAND_WIDE, k):
        rows = [_CAND_WIDE * k + i for i, (a, _) in enumerate(_CAND_TAIL) if a == p1]
        per_rank.append(jnp.sum(chosen[rows[0]:rows[-1] + 1], axis=0, keepdims=True))
    count = jnp.zeros(s1.shape, _F32)
    for p in range(k):
        count = jnp.where((rank1 == p) if exact else (s1 == top1[p:p + 1]), per_rank[p], count)
    z = jnp.sum(jnp.exp(top_c - top_c[0:1]), axis=0, keepdims=True)
    coef = jnp.exp(s1 - top1[0:1]) / z
    bad = None if exact else jnp.abs(_extracted(left_c) - (k + n_pad))
    return count, coef, bad


def _select_body(q_ref, k1_ref, k2_ref, cnt_ref, coef_ref, rank_ref, e2_ref, s1_scr, s2_scr):
    q = q_ref[...]
    s1_scr[...] = lax.dot_general(k1_ref[...], q[:, :PEER_HALF], _NT, preferred_element_type=_F32)
    s2_scr[...] = lax.dot_general(k2_ref[...], q[:, PEER_HALF:], _NT, preferred_element_type=_F32)
    n_tiles = s1_scr.shape[1] // LANES

    def lanes_of(i):
        return pl.ds(pl.multiple_of(i * LANES, LANES), LANES)

    def keys_stage(i, exact):
        lanes = lanes_of(i)
        top1, top2, rank1, rank2, e2, bad = _rank_keys(s1_scr[:, lanes], s2_scr[:, lanes], exact)
        rank_ref[:, lanes] = rank2.astype(rank_ref.dtype)
        e2_ref[:, lanes] = e2.astype(e2_ref.dtype)
        return top1, top2, rank1, bad

    def pairs_stage(i, top1, top2, rank1, exact):
        lanes = lanes_of(i)
        count, coef, bad = _rank_pairs(s1_scr[:, lanes], top1, top2, rank1, exact)
        cnt_ref[:, lanes] = count
        coef_ref[:, lanes] = coef
        return bad

    top1, top2, _, bad = keys_stage(0, exact=False)

    def skewed(i, carry):
        top1, top2, bad = carry
        nxt1, nxt2, _, bad_keys = keys_stage(i + 1, exact=False)
        bad_pairs = pairs_stage(i, top1, top2, None, exact=False)
        return nxt1, nxt2, jnp.maximum(bad, jnp.maximum(bad_keys, bad_pairs))

    top1, top2, bad = lax.fori_loop(0, n_tiles - 1, skewed, (top1, top2, bad))
    bad = jnp.maximum(bad, pairs_stage(n_tiles - 1, top1, top2, None, exact=False))

    @pl.when(jnp.max(bad) > 0.0)
    def _():
        def exact_tile(i, carry):
            top1, top2, rank1, _ = keys_stage(i, exact=True)
            pairs_stage(i, top1, top2, rank1, exact=True)
            return carry

        lax.fori_loop(0, n_tiles, exact_tile, 0)


def _select(q, keys1, keys2):
    t = q.shape[0]
    tm = min(512, t)
    qd = 2 * PEER_HALF
    out = jax.ShapeDtypeStruct((PEER_HEADS, N_KEYS, t), _F32)
    out16 = jax.ShapeDtypeStruct((PEER_HEADS, N_KEYS, t), _BF16)
    spec = pl.BlockSpec((None, N_KEYS, tm), lambda i, h: (h, 0, i))
    return pl.pallas_call(
        _select_body,
        out_shape=(out, out, out16, out16),
        grid=(t // tm, PEER_HEADS),
        in_specs=[
            pl.BlockSpec((tm, qd), lambda i, h: (i, h)),
            pl.BlockSpec((None, N_KEYS, PEER_HALF), lambda i, h: (h, 0, 0)),
            pl.BlockSpec((None, N_KEYS, PEER_HALF), lambda i, h: (h, 0, 0)),
        ],
        out_specs=(spec, spec, spec, spec),
        scratch_shapes=[pltpu.VMEM((N_KEYS, tm), _F32), pltpu.VMEM((N_KEYS, tm), _F32)],
        compiler_params=_params(("arbitrary", "arbitrary")),
        name="select",
    )(q, keys1, keys2)


def _peer_body(h2_ref, down_ref, upt_ref, cnt_ref, coef_ref, rank_ref, e2_ref, x1_ref, o_ref,
               acc_scr, wg_scr, *, key_rows):
    j = pl.program_id(1)

    @pl.when(j == 0)
    def _():
        acc_scr[...] = jnp.zeros_like(acc_scr)

    act = lax.dot_general(down_ref[...], h2_ref[...], _NT, preferred_element_type=_F32)
    gact = _gelu(act).astype(_BF16)
    block = (N_KEYS, gact.shape[1])
    zero = jnp.zeros(block, _BF16)
    for a in range(key_rows):
        rows = slice(a * N_KEYS, (a + 1) * N_KEYS)
        w = None
        for h in range(PEER_HEADS):
            cnt = jnp.broadcast_to(cnt_ref[h, a:a + 1, :].astype(_BF16), block)
            coef = jnp.broadcast_to(coef_ref[h, a:a + 1, :].astype(_BF16), block)
            term = jnp.where(rank_ref[h] < cnt, e2_ref[h], zero) * coef
            w = term if w is None else w + term
        wg_scr[rows, :] = w * gact[rows, :]
    acc_scr[...] += jnp.dot(upt_ref[...], wg_scr[...], preferred_element_type=_F32)

    @pl.when(j == pl.num_programs(1) - 1)
    def _():
        o_ref[...] = x1_ref[...] + acc_scr[...].T


def _peer(h2, down, up_t, count, coef, rank2, e2, x1):
    t, d = h2.shape
    n_e = down.shape[0]
    tm, te = min(512, t), 1024
    key_rows = te // N_KEYS
    once = pl.Buffered(1)
    return pl.pallas_call(
        functools.partial(_peer_body, key_rows=key_rows),
        out_shape=jax.ShapeDtypeStruct((t, d), _F32),
        grid=(t // tm, n_e // te),
        in_specs=[
            pl.BlockSpec((tm, d), lambda i, j: (i, 0), pipeline_mode=once),
            pl.BlockSpec((te, d), lambda i, j: (j, 0)),
            pl.BlockSpec((d, te), lambda i, j: (0, j)),
            pl.BlockSpec((PEER_HEADS, key_rows, tm), lambda i, j: (0, j, i)),
            pl.BlockSpec((PEER_HEADS, key_rows, tm), lambda i, j: (0, j, i)),
            pl.BlockSpec((PEER_HEADS, N_KEYS, tm), lambda i, j: (0, 0, i), pipeline_mode=once),
            pl.BlockSpec((PEER_HEADS, N_KEYS, tm), lambda i, j: (0, 0, i), pipeline_mode=once),
            pl.BlockSpec((tm, d), lambda i, j: (i, 0), pipeline_mode=once),
        ],
        out_specs=pl.BlockSpec((tm, d), lambda i, j: (i, 0)),
        scratch_shapes=[pltpu.VMEM((d, tm), _F32), pltpu.VMEM((te, tm), _BF16)],
        compiler_params=_params(("arbitrary", "arbitrary")),
        name="peer",
    )(h2, down, up_t, count, coef, rank2, e2, x1)


def _layer(x, mix_g, w_in, q_g, k_g, sink, w_attn_o, a_re, a_im, log_step, b_re, b_im, c_re, c_im, d_skip,
           glu_a, glu_b, w_out, ffn_g, wq, keys1, keys2, down, up):
    b, s, d = x.shape
    t = b * s
    aw, kw = N_Q_HEADS * HEAD_DIM, N_KV_HEADS * HEAD_DIM
    sw = d_skip.shape[0]
    n_g = sw // SSM_GROUP
    u_col = aw + 2 * kw
    ga_col = u_col + sw
    gb_col = ga_col + d
    x2 = x.reshape(t, d)

    proj = _inproj(x2, mix_g, w_in.astype(_BF16))
    attn = _attention(proj.reshape(b, s, -1), q_g, k_g, sink).reshape(t, aw)

    n_k = s // SSM_CHUNK
    u = proj[:, u_col:u_col + sw].reshape(b, n_k, SSM_CHUNK, n_g, SSM_GROUP)
    u_chunks = u.transpose(3, 1, 0, 2, 4).reshape(n_g, n_k * b, SSM_CHUNK * SSM_GROUP)
    m, s_in, s_out, lam4 = _ssm_matrices(a_re, a_im, log_step, b_re, b_im, c_re, c_im, d_skip)
    yact = _ssm(u_chunks, m, s_in, s_out, lam4, b)
    yact = yact.reshape(n_g, n_k, b, SSM_CHUNK, SSM_GROUP).transpose(2, 1, 3, 0, 4).reshape(t, sw)

    merged = _merge(attn, yact, proj, ga_col, gb_col,
                    w_attn_o.astype(_BF16), glu_a.astype(_BF16), glu_b.astype(_BF16))
    x1, h2, q = _outproj(x2, merged, w_out.astype(_BF16), ffn_g, wq.astype(_BF16))

    count, coef, rank2, e2 = _select(q, keys1.astype(_BF16), keys2.astype(_BF16))
    out = _peer(h2, down.astype(_BF16), up.T.astype(_BF16), count, coef, rank2, e2, x1)
    return out.reshape(b, s, d)


def kernel(x, mix_norm_g, w_in, q_norm_g, k_norm_g, attn_sink, w_attn_o, ssm_a_re, ssm_a_im, ssm_log_step,
           ssm_b_re, ssm_b_im, ssm_c_re, ssm_c_im, ssm_d, glu_w_a, glu_w_b, w_out, ffn_norm_g, peer_w_query,
           peer_sub_keys_1, peer_sub_keys_2, peer_down, peer_up):
    for l in range(mix_norm_g.shape[0]):
        x = _layer(x, mix_norm_g[l], w_in[l], q_norm_g[l], k_norm_g[l], attn_sink[l], w_attn_o[l],
                   ssm_a_re[l], ssm_a_im[l], ssm_log_step[l], ssm_b_re[l], ssm_b_im[l], ssm_c_re[l],
                   ssm_c_im[l], ssm_d[l], glu_w_a[l], glu_w_b[l], w_out[l], ffn_norm_g[l], peer_w_query[l],
                   peer_sub_keys_1[l], peer_sub_keys_2[l], peer_down[l], peer_up[l])
    return x
```

```python
import functools
import math

import jax
import jax.numpy as jnp
from jax import lax
from jax.experimental import pallas as pl
from jax.experimental.pallas import tpu as pltpu

N_Q_HEADS = 16
N_KV_HEADS = 4
Q_GROUP = N_Q_HEADS // N_KV_HEADS
HEAD_DIM = 128
WINDOW = 128
BLOCK = 128
SSM_GROUP = 16
SSM_STATE = 64
SSM_CHUNK = 16
PEER_HEADS = 8
N_KEYS = 128
PEER_HALF = 128
PEER_TOPK = 16
RMS_EPS = 1e-6
LANES = 128
VMEM_LIMIT = 56 * 1024 * 1024

_F32 = jnp.float32
_BF16 = jnp.bfloat16
_NT = (((1,), (1,)), ((), ()))


def _params(semantics):
    return pltpu.CompilerParams(dimension_semantics=semantics, vmem_limit_bytes=VMEM_LIMIT)


def _gelu(v):
    return 0.5 * v * (1.0 + lax.erf(v * (1.0 / math.sqrt(2.0))))


def _rms_scale(v):
    return lax.rsqrt(jnp.mean(v * v, axis=-1, keepdims=True) + RMS_EPS)


def _inproj_body(x_ref, g_ref, w_ref, o_ref, h_scr):
    @pl.when(pl.program_id(1) == 0)
    def _():
        x = x_ref[...]
        h_scr[...] = (x * _rms_scale(x) * g_ref[...]).astype(h_scr.dtype)

    o_ref[...] = jnp.dot(h_scr[...], w_ref[...], preferred_element_type=_F32).astype(o_ref.dtype)


def _inproj(x2, gain, w_bf16):
    t, d = x2.shape
    n = w_bf16.shape[1]
    tm, tn = min(1024, t), 1024
    return pl.pallas_call(
        _inproj_body,
        out_shape=jax.ShapeDtypeStruct((t, n), _BF16),
        grid=(t // tm, n // tn),
        in_specs=[
            pl.BlockSpec((tm, d), lambda i, j: (i, 0)),
            pl.BlockSpec((1, d), lambda i, j: (0, 0)),
            pl.BlockSpec((d, tn), lambda i, j: (0, j)),
        ],
        out_specs=pl.BlockSpec((tm, tn), lambda i, j: (i, j)),
        scratch_shapes=[pltpu.VMEM((tm, d), _BF16)],
        compiler_params=_params(("arbitrary", "arbitrary")),
        name="inproj",
    )(x2, gain.reshape(1, d), w_bf16)


def _attn_body(sink_ref, q_ref, kp_ref, kc_ref, kn_ref, vp_ref, vc_ref, vn_ref, qg_ref, kg_ref, o_ref,
               *, seq_len, slopes):
    i = pl.program_id(1)
    row = lax.broadcasted_iota(jnp.int32, (BLOCK, 3 * BLOCK), 0)
    col = lax.broadcasted_iota(jnp.int32, (BLOCK, 3 * BLOCK), 1)
    dist = jnp.abs(row + BLOCK - col)
    kpos = (i - 1) * BLOCK + col
    valid = (dist <= WINDOW) & (kpos >= 0) & (kpos < seq_len)
    neg_dist = jnp.where(valid, -dist.astype(_F32), -jnp.inf)

    k3 = jnp.concatenate([kp_ref[...], kc_ref[...], kn_ref[...]], axis=0).astype(_F32)
    v3 = jnp.concatenate([vp_ref[...], vc_ref[...], vn_ref[...]], axis=0)
    qf = q_ref[...].astype(_F32)
    q_gain = qg_ref[...] * (HEAD_DIM ** -0.5)
    k_gain = kg_ref[...]

    for kv in range(N_KV_HEADS):
        kh = k3[:, kv * HEAD_DIM:(kv + 1) * HEAD_DIM]
        kh = (kh * _rms_scale(kh) * k_gain).astype(_BF16)
        q_rows = []
        for g in range(Q_GROUP):
            h = kv * Q_GROUP + g
            qh = qf[:, h * HEAD_DIM:(h + 1) * HEAD_DIM]
            q_rows.append((qh * _rms_scale(qh) * q_gain).astype(_BF16))
        scores = lax.dot_general(jnp.concatenate(q_rows, axis=0), kh, _NT,
                                 preferred_element_type=_F32)
        probs, denoms = [], []
        for g in range(Q_GROUP):
            h = kv * Q_GROUP + g
            s = scores[g * BLOCK:(g + 1) * BLOCK] + slopes[h] * neg_dist
            sink = sink_ref[h]
            m = jnp.maximum(jnp.max(s, axis=-1, keepdims=True), sink)
            p = jnp.exp(s - m)
            denoms.append(jnp.sum(p, axis=-1, keepdims=True) + jnp.exp(sink - m))
            probs.append(p.astype(_BF16))
        pv = jnp.dot(jnp.concatenate(probs, axis=0), v3[:, kv * HEAD_DIM:(kv + 1) * HEAD_DIM],
                     preferred_element_type=_F32)
        for g in range(Q_GROUP):
            h = kv * Q_GROUP + g
            o_ref[:, h * HEAD_DIM:(h + 1) * HEAD_DIM] = (
                pv[g * BLOCK:(g + 1) * BLOCK] / denoms[g]).astype(o_ref.dtype)


def _attention(proj3, q_gain, k_gain, sink):
    b, s, _ = proj3.shape
    nb = s // BLOCK
    aw, kw = N_Q_HEADS * HEAD_DIM, N_KV_HEADS * HEAD_DIM
    k_col, v_col = aw // kw, aw // kw + 1
    slopes = tuple(2.0 ** (-8.0 * (h + 1.0) / N_Q_HEADS) for h in range(N_Q_HEADS))

    def kv_spec(col, shift):
        return pl.BlockSpec((None, BLOCK, kw),
                            lambda bi, i: (bi, jnp.clip(i + shift, 0, nb - 1), col))

    return pl.pallas_call(
        functools.partial(_attn_body, seq_len=s, slopes=slopes),
        out_shape=jax.ShapeDtypeStruct((b, s, aw), _BF16),
        grid=(b, nb),
        in_specs=[
            pl.BlockSpec(memory_space=pltpu.SMEM),
            pl.BlockSpec((None, BLOCK, aw), lambda bi, i: (bi, i, 0)),
            kv_spec(k_col, -1), kv_spec(k_col, 0), kv_spec(k_col, 1),
            kv_spec(v_col, -1), kv_spec(v_col, 0), kv_spec(v_col, 1),
            pl.BlockSpec((1, HEAD_DIM), lambda bi, i: (0, 0)),
            pl.BlockSpec((1, HEAD_DIM), lambda bi, i: (0, 0)),
        ],
        out_specs=pl.BlockSpec((None, BLOCK, aw), lambda bi, i: (bi, i, 0)),
        compiler_params=_params(("arbitrary", "arbitrary")),
        name="attn",
    )(sink.astype(_F32), proj3, proj3, proj3, proj3, proj3, proj3, proj3,
      q_gain.reshape(1, HEAD_DIM).astype(_F32), k_gain.reshape(1, HEAD_DIM).astype(_F32))


def _ssm_matrices(a_re, a_im, log_step, b_re, b_im, c_re, c_im, d_skip):
    hp = lax.Precision.HIGHEST
    n_l, n_c, n_s = SSM_CHUNK, SSM_GROUP, SSM_STATE
    n_g = a_re.shape[1]
    a_re, a_im, b_re, b_im, c_re, c_im = (v.astype(_F32) for v in (a_re, a_im, b_re, b_im, c_re, c_im))
    step = jnp.exp(log_step.astype(_F32))[..., None]
    steps = jnp.arange(n_l + 1, dtype=_F32)[:, None, None, None]
    mag = jnp.exp((a_re * step)[None] * steps)
    ang = (a_im * step)[None] * steps
    p_re, p_im = mag * jnp.cos(ang), mag * jnp.sin(ang)
    num_re, num_im = p_re[1] - 1.0, p_im[1]
    den = a_re * a_re + a_im * a_im
    f_re = ((num_re * a_re + num_im * a_im) / den)[..., None]
    f_im = ((num_im * a_re - num_re * a_im) / den)[..., None]
    bb_re, bb_im = f_re * b_re - f_im * b_im, f_re * b_im + f_im * b_re
    pe_re, pe_im = p_re[:, :, :, None, :], p_im[:, :, :, None, :]
    cp_re = c_re[None] * pe_re - c_im[None] * pe_im
    cp_im = c_re[None] * pe_im + c_im[None] * pe_re

    kern = (jnp.einsum('tdgcn,dgne->dtgce', cp_re[:n_l], bb_re, precision=hp)
            - jnp.einsum('tdgcn,dgne->dtgce', cp_im[:n_l], bb_im, precision=hp))
    s_idx = jnp.arange(n_l)[:, None]
    t_idx = jnp.arange(n_l)[None, :]
    tau_f, tau_b = t_idx - s_idx, s_idx - t_idx
    resp = (jnp.where((tau_f >= 0)[:, :, None, None, None], kern[0][jnp.clip(tau_f, 0, n_l - 1)], 0.0)
            + jnp.where((tau_b >= 0)[:, :, None, None, None], kern[1][jnp.clip(tau_b, 0, n_l - 1)], 0.0))
    skip = (jnp.eye(n_l, dtype=_F32)[:, :, None, None, None]
            * (jnp.eye(n_c, dtype=_F32)[None, None, None] * d_skip.astype(_F32).reshape(1, 1, n_g, n_c, 1)))
    m_g = (resp + skip).transpose(2, 0, 4, 1, 3)

    def state_in(direction, powers):
        q_re, q_im = p_re[powers, direction][:, :, None, :], p_im[powers, direction][:, :, None, :]
        t_re, t_im = bb_re[direction].transpose(0, 2, 1)[None], bb_im[direction].transpose(0, 2, 1)[None]
        return q_re * t_re - q_im * t_im, q_re * t_im + q_im * t_re

    w_in_g = jnp.stack(state_in(0, n_l - 1 - jnp.arange(n_l)) + state_in(1, jnp.arange(n_l)), axis=3)
    w_in_g = w_in_g.transpose(1, 0, 2, 3, 4)
    out_f, out_b = 1 + jnp.arange(n_l), n_l - jnp.arange(n_l)
    w_out_g = jnp.stack([z.transpose(1, 3, 0, 2) for z in
                         (cp_re[out_f, 0], -cp_im[out_f, 0], cp_re[out_b, 1], -cp_im[out_b, 1])], axis=1)
    lam_g = jnp.stack([p_re[n_l, 0], p_im[n_l, 0], p_re[n_l, 1], p_im[n_l, 1]], axis=1)

    gp = LANES // n_c
    n_sg = n_g // gp
    width = n_l * gp * n_c
    eye = jnp.eye(gp, dtype=_F32)[None, :, None, None, None, :, None]

    def tile_diag(z):
        return (z[:, :, :, :, :, None, :] * eye).transpose(0, 2, 1, 3, 4, 5, 6)

    m = tile_diag(m_g.reshape(n_sg, gp, n_l, n_c, n_l, n_c)).reshape(n_sg, width, width)
    w_in = tile_diag(w_in_g.reshape(n_sg, gp, n_l, n_c, 4, n_s)).reshape(n_sg, width, 4 * gp * n_s)
    w_out = tile_diag(w_out_g.reshape(n_sg, gp, 4, n_s, n_l, n_c)).reshape(n_sg, 4 * gp * n_s, width)
    lam = lam_g.reshape(n_sg, gp, 4, n_s).transpose(0, 2, 1, 3).reshape(n_sg, 4, gp * n_s)
    return m.astype(_BF16), w_in.astype(_BF16), w_out.astype(_BF16), lam


def _ssm_body(x_ref, m_ref, win_ref, wout_ref, lam_ref, o_ref, lhs_scr, st_scr, io_scr, *, n_chunks, n_batch):
    n_l = SSM_CHUNK
    rows = n_batch * n_chunks
    pairs = n_l // 2
    hi = jnp.uint32(0xFFFF0000)
    io_scr[...] = pltpu.bitcast(x_ref[...], jnp.uint32)
    for jj in range(pairs):
        w = io_scr[pl.ds(jj, rows, stride=pairs), :]
        even = pltpu.bitcast(w << 16, _F32)
        odd = pltpu.bitcast(w & hi, _F32)
        lhs_scr[:, (2 * jj) * LANES:(2 * jj + 1) * LANES] = even.astype(_BF16)
        lhs_scr[:, (2 * jj + 1) * LANES:(2 * jj + 2) * LANES] = odd.astype(_BF16)

    inc = jnp.dot(lhs_scr[...], win_ref[...], preferred_element_type=_F32)
    n_tiles = st_scr.shape[0]
    q = n_tiles // 4
    for t in range(n_tiles):
        st_scr[t] = inc[:, t * LANES:(t + 1) * LANES]
    lam = lam_ref[...]
    coef = [[jnp.broadcast_to(lam[p:p + 1, i * LANES:(i + 1) * LANES], (n_batch, LANES)) for i in range(q)]
            for p in range(4)]

    def step(k, carry):
        rf = pl.ds(k, n_batch, stride=n_chunks)
        rb = pl.ds(n_chunks - 1 - k, n_batch, stride=n_chunks)
        new = list(carry)
        for i in range(q):
            for base, r in ((0, rf), (2 * q, rb)):
                t_re, t_im = base + i, base + q + i
                a_re, a_im = coef[base // q][i], coef[base // q + 1][i]
                s_re, s_im = carry[t_re], carry[t_im]
                d_re, d_im = st_scr[t_re, r, :], st_scr[t_im, r, :]
                st_scr[t_re, r, :] = s_re
                st_scr[t_im, r, :] = s_im
                new[t_re] = a_re * s_re - a_im * s_im + d_re
                new[t_im] = a_re * s_im + a_im * s_re + d_im
        return tuple(new)

    zero = jnp.zeros((n_batch, LANES), _F32)
    lax.fori_loop(0, n_chunks, step, (zero,) * n_tiles, unroll=True)
    states = jnp.concatenate([st_scr[t] for t in range(n_tiles)], axis=1).astype(_BF16)
    y = (jnp.dot(lhs_scr[...], m_ref[...], preferred_element_type=_F32)
         + jnp.dot(states, wout_ref[...], preferred_element_type=_F32))
    y = _gelu(y).astype(_BF16).astype(_F32)

    for jj in range(pairs):
        even = pltpu.bitcast(y[:, (2 * jj) * LANES:(2 * jj + 1) * LANES], jnp.uint32)
        odd = pltpu.bitcast(y[:, (2 * jj + 1) * LANES:(2 * jj + 2) * LANES], jnp.uint32)
        io_scr[pl.ds(jj, rows, stride=pairs), :] = (odd & hi) | (even >> 16)
    o_ref[...] = pltpu.bitcast(io_scr[...], _BF16)


def _ssm(proj, u_col, m, w_in, w_out, lam, n_batch_total, seq_len):
    t = proj.shape[0]
    n_sg, width, _ = m.shape
    bt = 4
    n_chunks = seq_len // SSM_CHUNK
    rows = bt * n_chunks
    n_tiles = w_in.shape[2] // LANES
    once = pl.Buffered(1)
    col0 = u_col // LANES
    return pl.pallas_call(
        functools.partial(_ssm_body, n_chunks=n_chunks, n_batch=bt),
        out_shape=jax.ShapeDtypeStruct((t, n_sg * LANES), _BF16),
        grid=(n_sg, n_batch_total // bt),
        in_specs=[
            pl.BlockSpec((bt * seq_len, LANES), lambda s, i: (i, col0 + s)),
            pl.BlockSpec((None, width, width), lambda s, i: (s, 0, 0), pipeline_mode=once),
            pl.BlockSpec((None, width, n_tiles * LANES), lambda s, i: (s, 0, 0), pipeline_mode=once),
            pl.BlockSpec((None, n_tiles * LANES, width), lambda s, i: (s, 0, 0), pipeline_mode=once),
            pl.BlockSpec((None, 4, n_tiles * LANES // 4), lambda s, i: (s, 0, 0)),
        ],
        out_specs=pl.BlockSpec((bt * seq_len, LANES), lambda s, i: (i, s)),
        scratch_shapes=[pltpu.VMEM((rows, width), _BF16), pltpu.VMEM((n_tiles, rows, LANES), _F32),
                        pltpu.VMEM((bt * seq_len // 2, LANES), jnp.uint32)],
        compiler_params=_params(("arbitrary", "arbitrary")),
        name="ssm",
    )(proj, m, w_in, w_out, lam)


def _merge_body(attn_ref, yact_ref, ga_ref, gb_ref, wo_ref, wa_ref, wb_ref, o_ref):
    y_a = jnp.dot(attn_ref[...], wo_ref[...], preferred_element_type=_F32)
    yact = yact_ref[...]
    glu = (jnp.dot(yact, wa_ref[...], preferred_element_type=_F32)
           * jax.nn.sigmoid(jnp.dot(yact, wb_ref[...], preferred_element_type=_F32)))
    o_ref[...] = (jax.nn.sigmoid(ga_ref[...].astype(_F32)) * y_a
                  + jax.nn.sigmoid(gb_ref[...].astype(_F32)) * glu).astype(o_ref.dtype)


def _merge(attn2, yact2, proj, gate_a_col, gate_b_col, wo, wa, wb):
    t, aw = attn2.shape
    sw = yact2.shape[1]
    d = wo.shape[1]
    tm, tn = min(512, t), d
    ga0, gb0 = gate_a_col // tn, gate_b_col // tn
    once = pl.Buffered(1)
    return pl.pallas_call(
        _merge_body,
        out_shape=jax.ShapeDtypeStruct((t, d), _BF16),
        grid=(t // tm, d // tn),
        in_specs=[
            pl.BlockSpec((tm, aw), lambda i, j: (i, 0)),
            pl.BlockSpec((tm, sw), lambda i, j: (i, 0)),
            pl.BlockSpec((tm, tn), lambda i, j: (i, ga0 + j)),
            pl.BlockSpec((tm, tn), lambda i, j: (i, gb0 + j)),
            pl.BlockSpec((aw, tn), lambda i, j: (0, j), pipeline_mode=once),
            pl.BlockSpec((sw, tn), lambda i, j: (0, j), pipeline_mode=once),
            pl.BlockSpec((sw, tn), lambda i, j: (0, j), pipeline_mode=once),
        ],
        out_specs=pl.BlockSpec((tm, tn), lambda i, j: (i, j)),
        compiler_params=_params(("arbitrary", "arbitrary")),
        name="merge",
    )(attn2, yact2, proj, proj, wo, wa, wb)


def _outproj_body(x_ref, m_ref, w_ref, g_ref, wq_ref, x1_ref, h2_ref, q_ref):
    x1 = x_ref[...] + jnp.dot(m_ref[...], w_ref[...], preferred_element_type=_F32)
    x1_ref[...] = x1
    h2 = (x1 * _rms_scale(x1) * g_ref[...]).astype(h2_ref.dtype)
    h2_ref[...] = h2
    q_ref[...] = jnp.dot(h2, wq_ref[...], preferred_element_type=_F32).astype(q_ref.dtype)


def _outproj(x2, merged, w_out, gain, wq):
    t, d = x2.shape
    qw = wq.shape[1]
    tm = min(512, t)
    rows = lambda width: pl.BlockSpec((tm, width), lambda i: (i, 0))
    whole = lambda arr: pl.BlockSpec(arr.shape, lambda i: (0, 0))
    return pl.pallas_call(
        _outproj_body,
        out_shape=(jax.ShapeDtypeStruct((t, d), _F32), jax.ShapeDtypeStruct((t, d), _BF16),
                   jax.ShapeDtypeStruct((t, qw), _BF16)),
        grid=(t // tm,),
        in_specs=[rows(d), rows(d), whole(w_out), pl.BlockSpec((1, d), lambda i: (0, 0)), whole(wq)],
        out_specs=(rows(d), rows(d), rows(qw)),
        compiler_params=_params(("arbitrary",)),
        name="outproj",
    )(x2, merged, w_out, gain.reshape(1, d), wq)


_CAND_WIDE = 4
_CAND_TAIL = tuple((p1, p2) for p1 in range(_CAND_WIDE, PEER_TOPK) for p2 in range(PEER_TOPK)
                   if (p1 + 1) * (p2 + 1) <= PEER_TOPK)
_SUBLANES = 8


def _top_rows(vals, k, exact, want_rank):
    n = vals.shape[0]
    row = lax.broadcasted_iota(jnp.int32, vals.shape, 0).astype(_F32) if exact else None
    rank = jnp.full(vals.shape, float(k), _F32) if want_rank else None
    work = vals
    tops = []
    for r in range(k):
        best = jnp.max(work, axis=0, keepdims=True)
        hit = work == best
        if exact:
            first = jnp.min(jnp.where(hit, row, float(n)), axis=0, keepdims=True)
            hit = row == first
        if want_rank:
            rank = jnp.where(hit, float(r), rank)
        work = jnp.where(hit, -jnp.inf, work)
        tops.append(best)
    return jnp.concatenate(tops, axis=0), rank, work


def _extracted(left):
    return jnp.sum(jnp.where(left == -jnp.inf, 1.0, 0.0), axis=0, keepdims=True)


def _rank_keys(s1, s2, exact):
    k = PEER_TOPK
    top1, rank1, left1 = _top_rows(s1, k, exact, want_rank=exact)
    top2, rank2, left2 = _top_rows(s2, k, exact, want_rank=True)
    e2 = jnp.exp(s2 - top2[0:1])
    bad = None if exact else jnp.abs(_extracted(left1) - k) + jnp.abs(_extracted(left2) - k)
    return top1, top2, rank1, rank2, e2, bad


def _rank_pairs(s1, top1, top2, rank1, exact):
    k = PEER_TOPK
    n_pad = -len(_CAND_TAIL) % _SUBLANES
    pad = [jnp.full_like(top1[0:1], -jnp.inf)] * n_pad
    cand = jnp.concatenate([top1[p:p + 1] + top2 for p in range(_CAND_WIDE)]
                           + [top1[p1:p1 + 1] + top2[p2:p2 + 1] for p1, p2 in _CAND_TAIL] + pad, axis=0)
    top_c, _, left_c = _top_rows(cand, k, exact, want_rank=False)
    chosen = jnp.where(left_c == -jnp.inf, 1.0, 0.0)
    per_rank = [jnp.sum(chosen[p * k:(p + 1) * k], axis=0, keepdims=True) for p in range(_CAND_WIDE)]
    for p1 in range(_CAND_WIDE, k):
        rows = [_CAND_WIDE * k + i for i, (a, _) in enumerate(_CAND_TAIL) if a == p1]
        per_rank.append(jnp.sum(chosen[rows[0]:rows[-1] + 1], axis=0, keepdims=True))
    count = jnp.zeros(s1.shape, _F32)
    for p in range(k):
        count = jnp.where((rank1 == p) if exact else (s1 == top1[p:p + 1]), per_rank[p], count)
    z = jnp.sum(jnp.exp(top_c - top_c[0:1]), axis=0, keepdims=True)
    coef = jnp.exp(s1 - top1[0:1]) / z
    bad = None if exact else jnp.abs(_extracted(left_c) - (k + n_pad))
    return count, coef, bad


def _select_body(q_ref, k1_ref, k2_ref, cnt_ref, coef_ref, rank_ref, e2_ref, s1_scr, s2_scr):
    q = q_ref[...]
    s1_scr[...] = lax.dot_general(k1_ref[...], q[:, :PEER_HALF], _NT, preferred_element_type=_F32)
    s2_scr[...] = lax.dot_general(k2_ref[...], q[:, PEER_HALF:], _NT, preferred_element_type=_F32)
    n_tiles = s1_scr.shape[1] // LANES

    def lanes_of(i):
        return pl.ds(pl.multiple_of(i * LANES, LANES), LANES)

    def keys_stage(i, exact):
        lanes = lanes_of(i)
        top1, top2, rank1, rank2, e2, bad = _rank_keys(s1_scr[:, lanes], s2_scr[:, lanes], exact)
        rank_ref[:, lanes] = rank2.astype(rank_ref.dtype)
        e2_ref[:, lanes] = e2.astype(e2_ref.dtype)
        return top1, top2, rank1, bad

    def pairs_stage(i, top1, top2, rank1, exact):
        lanes = lanes_of(i)
        count, coef, bad = _rank_pairs(s1_scr[:, lanes], top1, top2, rank1, exact)
        cnt_ref[:, lanes] = count
        coef_ref[:, lanes] = coef
        return bad

    top1, top2, _, bad = keys_stage(0, exact=False)

    def skewed(i, carry):
        top1, top2, bad = carry
        nxt1, nxt2, _, bad_keys = keys_stage(i + 1, exact=False)
        bad_pairs = pairs_stage(i, top1, top2, None, exact=False)
        return nxt1, nxt2, jnp.maximum(bad, jnp.maximum(bad_keys, bad_pairs))

    top1, top2, bad = lax.fori_loop(0, n_tiles - 1, skewed, (top1, top2, bad))
    bad = jnp.maximum(bad, pairs_stage(n_tiles - 1, top1, top2, None, exact=False))

    @pl.when(jnp.max(bad) > 0.0)
    def _():
        def exact_tile(i, carry):
            top1, top2, rank1, _ = keys_stage(i, exact=True)
            pairs_stage(i, top1, top2, rank1, exact=True)
            return carry

        lax.fori_loop(0, n_tiles, exact_tile, 0)


def _select(q, keys1, keys2):
    t = q.shape[0]
    tm = min(512, t)
    qd = 2 * PEER_HALF
    out = jax.ShapeDtypeStruct((PEER_HEADS, N_KEYS, t), _F32)
    out16 = jax.ShapeDtypeStruct((PEER_HEADS, N_KEYS, t), _BF16)
    spec = pl.BlockSpec((None, N_KEYS, tm), lambda i, h: (h, 0, i))
    return pl.pallas_call(
        _select_body,
        out_shape=(out, out, out16, out16),
        grid=(t // tm, PEER_HEADS),
        in_specs=[
            pl.BlockSpec((tm, qd), lambda i, h: (i, h)),
            pl.BlockSpec((None, N_KEYS, PEER_HALF), lambda i, h: (h, 0, 0)),
            pl.BlockSpec((None, N_KEYS, PEER_HALF), lambda i, h: (h, 0, 0)),
        ],
        out_specs=(spec, spec, spec, spec),
        scratch_shapes=[pltpu.VMEM((N_KEYS, tm), _F32), pltpu.VMEM((N_KEYS, tm), _F32)],
        compiler_params=_params(("arbitrary", "arbitrary")),
        name="select",
    )(q, keys1, keys2)


def _peer_body(h2_ref, down_ref, upt_ref, cnt_ref, coef_ref, rank_ref, e2_ref, x1_ref, o_ref,
               acc_scr, wg_scr, *, key_rows):
    j = pl.program_id(1)

    @pl.when(j == 0)
    def _():
        acc_scr[...] = jnp.zeros_like(acc_scr)

    act = lax.dot_general(down_ref[...], h2_ref[...], _NT, preferred_element_type=_F32)
    gact = _gelu(act).astype(_BF16)
    block = (N_KEYS, gact.shape[1])
    zero = jnp.zeros(block, _BF16)
    for a in range(key_rows):
        rows = slice(a * N_KEYS, (a + 1) * N_KEYS)
        w = None
        for h in range(PEER_HEADS):
            cnt = jnp.broadcast_to(cnt_ref[h, a:a + 1, :].astype(_BF16), block)
            coef = jnp.broadcast_to(coef_ref[h, a:a + 1, :].astype(_BF16), block)
            term = jnp.where(rank_ref[h] < cnt, e2_ref[h], zero) * coef
            w = term if w is None else w + term
        wg_scr[rows, :] = w * gact[rows, :]
    acc_scr[...] += jnp.dot(upt_ref[...], wg_scr[...], preferred_element_type=_F32)

    @pl.when(j == pl.num_programs(1) - 1)
    def _():
        o_ref[...] = x1_ref[...] + acc_scr[...].T


def _peer(h2, down, up_t, count, coef, rank2, e2, x1):
    t, d = h2.shape
    n_e = down.shape[0]
    tm, te = min(512, t), 1024
    key_rows = te // N_KEYS
    once = pl.Buffered(1)
    return pl.pallas_call(
        functools.partial(_peer_body, key_rows=key_rows),
        out_shape=jax.ShapeDtypeStruct((t, d), _F32),
        grid=(t // tm, n_e // te),
        in_specs=[
            pl.BlockSpec((tm, d), lambda i, j: (i, 0), pipeline_mode=once),
            pl.BlockSpec((te, d), lambda i, j: (j, 0)),
            pl.BlockSpec((d, te), lambda i, j: (0, j)),
            pl.BlockSpec((PEER_HEADS, key_rows, tm), lambda i, j: (0, j, i)),
            pl.BlockSpec((PEER_HEADS, key_rows, tm), lambda i, j: (0, j, i)),
            pl.BlockSpec((PEER_HEADS, N_KEYS, tm), lambda i, j: (0, 0, i), pipeline_mode=once),
            pl.BlockSpec((PEER_HEADS, N_KEYS, tm), lambda i, j: (0, 0, i), pipeline_mode=once),
            pl.BlockSpec((tm, d), lambda i, j: (i, 0), pipeline_mode=once),
        ],
        out_specs=pl.BlockSpec((tm, d), lambda i, j: (i, 0)),
        scratch_shapes=[pltpu.VMEM((d, tm), _F32), pltpu.VMEM((te, tm), _BF16)],
        compiler_params=_params(("arbitrary", "arbitrary")),
        name="peer",
    )(h2, down, up_t, count, coef, rank2, e2, x1)


def _layer(x, mix_g, w_in, q_g, k_g, sink, w_attn_o, a_re, a_im, log_step, b_re, b_im, c_re, c_im, d_skip,
           glu_a, glu_b, w_out, ffn_g, wq, keys1, keys2, down, up):
    b, s, d = x.shape
    t = b * s
    aw, kw = N_Q_HEADS * HEAD_DIM, N_KV_HEADS * HEAD_DIM
    sw = d_skip.shape[0]
    u_col = aw + 2 * kw
    ga_col = u_col + sw
    gb_col = ga_col + d
    x2 = x.reshape(t, d)

    proj = _inproj(x2, mix_g, w_in.astype(_BF16))
    attn = _attention(proj.reshape(b, s, -1), q_g, k_g, sink).reshape(t, aw)

    m, s_in, s_out, lam = _ssm_matrices(a_re, a_im, log_step, b_re, b_im, c_re, c_im, d_skip)
    yact = _ssm(proj, u_col, m, s_in, s_out, lam, b, s)

    merged = _merge(attn, yact, proj, ga_col, gb_col,
                    w_attn_o.astype(_BF16), glu_a.astype(_BF16), glu_b.astype(_BF16))
    x1, h2, q = _outproj(x2, merged, w_out.astype(_BF16), ffn_g, wq.astype(_BF16))

    count, coef, rank2, e2 = _select(q, keys1.astype(_BF16), keys2.astype(_BF16))
    out = _peer(h2, down.astype(_BF16), up.T.astype(_BF16), count, coef, rank2, e2, x1)
    return out.reshape(b, s, d)


def kernel(x, mix_norm_g, w_in, q_norm_g, k_norm_g, attn_sink, w_attn_o, ssm_a_re, ssm_a_im, ssm_log_step,
           ssm_b_re, ssm_b_im, ssm_c_re, ssm_c_im, ssm_d, glu_w_a, glu_w_b, w_out, ffn_norm_g, peer_w_query,
           peer_sub_keys_1, peer_sub_keys_2, peer_down, peer_up):
    for l in range(mix_norm_g.shape[0]):
        x = _layer(x, mix_norm_g[l], w_in[l], q_norm_g[l], k_norm_g[l], attn_sink[l], w_attn_o[l],
                   ssm_a_re[l], ssm_a_im[l], ssm_log_step[l], ssm_b_re[l], ssm_b_im[l], ssm_c_re[l],
                   ssm_c_im[l], ssm_d[l], glu_w_a[l], glu_w_b[l], w_out[l], ffn_norm_g[l], peer_w_query[l],
                   peer_sub_keys_1[l], peer_sub_keys_2[l], peer_down[l], peer_up[l])
    return x
```

```python
import functools
import math

import jax
import jax.numpy as jnp
from jax import lax
from jax.experimental import pallas as pl
from jax.experimental.pallas import tpu as pltpu

N_Q_HEADS = 16
N_KV_HEADS = 4
Q_GROUP = N_Q_HEADS // N_KV_HEADS
HEAD_DIM = 128
WINDOW = 128
BLOCK = 128
SSM_GROUP = 16
SSM_STATE = 64
SSM_CHUNK = 16
PEER_HEADS = 8
N_KEYS = 128
PEER_HALF = 128
PEER_TOPK = 16
RMS_EPS = 1e-6
LANES = 128
VMEM_LIMIT = 56 * 1024 * 1024

_F32 = jnp.float32
_BF16 = jnp.bfloat16
_NT = (((1,), (1,)), ((), ()))


def _params(semantics):
    return pltpu.CompilerParams(dimension_semantics=semantics, vmem_limit_bytes=VMEM_LIMIT)


def _gelu(v):
    return 0.5 * v * (1.0 + lax.erf(v * (1.0 / math.sqrt(2.0))))


def _rms_scale(v):
    return lax.rsqrt(jnp.mean(v * v, axis=-1, keepdims=True) + RMS_EPS)


def _inproj_body(x_ref, g_ref, w_ref, o_ref, h_scr):
    @pl.when(pl.program_id(1) == 0)
    def _():
        x = x_ref[...]
        h_scr[...] = (x * _rms_scale(x) * g_ref[...]).astype(h_scr.dtype)

    o_ref[...] = jnp.dot(h_scr[...], w_ref[...], preferred_element_type=_F32).astype(o_ref.dtype)


def _inproj(x2, gain, w_bf16):
    t, d = x2.shape
    n = w_bf16.shape[1]
    tm, tn = min(1024, t), 1024
    return pl.pallas_call(
        _inproj_body,
        out_shape=jax.ShapeDtypeStruct((t, n), _BF16),
        grid=(t // tm, n // tn),
        in_specs=[
            pl.BlockSpec((tm, d), lambda i, j: (i, 0)),
            pl.BlockSpec((1, d), lambda i, j: (0, 0)),
            pl.BlockSpec((d, tn), lambda i, j: (0, j)),
        ],
        out_specs=pl.BlockSpec((tm, tn), lambda i, j: (i, j)),
        scratch_shapes=[pltpu.VMEM((tm, d), _BF16)],
        compiler_params=_params(("arbitrary", "arbitrary")),
        name="inproj",
    )(x2, gain.reshape(1, d), w_bf16)


def _attn_body(sink_ref, q_ref, kp_ref, kc_ref, kn_ref, vp_ref, vc_ref, vn_ref, qg_ref, kg_ref, o_ref,
               *, seq_len, slopes):
    i = pl.program_id(1)
    row = lax.broadcasted_iota(jnp.int32, (BLOCK, 3 * BLOCK), 0)
    col = lax.broadcasted_iota(jnp.int32, (BLOCK, 3 * BLOCK), 1)
    dist = jnp.abs(row + BLOCK - col)
    kpos = (i - 1) * BLOCK + col
    valid = (dist <= WINDOW) & (kpos >= 0) & (kpos < seq_len)
    neg_dist = jnp.where(valid, -dist.astype(_F32), -jnp.inf)

    k3 = jnp.concatenate([kp_ref[...], kc_ref[...], kn_ref[...]], axis=0).astype(_F32)
    v3 = jnp.concatenate([vp_ref[...], vc_ref[...], vn_ref[...]], axis=0)
    qf = q_ref[...].astype(_F32)
    q_gain = qg_ref[...] * (HEAD_DIM ** -0.5)
    k_gain = kg_ref[...]

    for kv in range(N_KV_HEADS):
        kh = k3[:, kv * HEAD_DIM:(kv + 1) * HEAD_DIM]
        kh = (kh * _rms_scale(kh) * k_gain).astype(_BF16)
        q_rows = []
        for g in range(Q_GROUP):
            h = kv * Q_GROUP + g
            qh = qf[:, h * HEAD_DIM:(h + 1) * HEAD_DIM]
            q_rows.append((qh * _rms_scale(qh) * q_gain).astype(_BF16))
        scores = lax.dot_general(jnp.concatenate(q_rows, axis=0), kh, _NT,
                                 preferred_element_type=_F32)
        probs, denoms = [], []
        for g in range(Q_GROUP):
            h = kv * Q_GROUP + g
            s = scores[g * BLOCK:(g + 1) * BLOCK] + slopes[h] * neg_dist
            sink = sink_ref[h]
            m = jnp.maximum(jnp.max(s, axis=-1, keepdims=True), sink)
            p = jnp.exp(s - m)
            denoms.append(jnp.sum(p, axis=-1, keepdims=True) + jnp.exp(sink - m))
            probs.append(p.astype(_BF16))
        pv = jnp.dot(jnp.concatenate(probs, axis=0), v3[:, kv * HEAD_DIM:(kv + 1) * HEAD_DIM],
                     preferred_element_type=_F32)
        for g in range(Q_GROUP):
            h = kv * Q_GROUP + g
            o_ref[:, h * HEAD_DIM:(h + 1) * HEAD_DIM] = (
                pv[g * BLOCK:(g + 1) * BLOCK] / denoms[g]).astype(o_ref.dtype)


def _attention(proj3, q_gain, k_gain, sink):
    b, s, _ = proj3.shape
    nb = s // BLOCK
    aw, kw = N_Q_HEADS * HEAD_DIM, N_KV_HEADS * HEAD_DIM
    k_col, v_col = aw // kw, aw // kw + 1
    slopes = tuple(2.0 ** (-8.0 * (h + 1.0) / N_Q_HEADS) for h in range(N_Q_HEADS))

    def kv_spec(col, shift):
        return pl.BlockSpec((None, BLOCK, kw),
                            lambda bi, i: (bi, jnp.clip(i + shift, 0, nb - 1), col))

    return pl.pallas_call(
        functools.partial(_attn_body, seq_len=s, slopes=slopes),
        out_shape=jax.ShapeDtypeStruct((b, s, aw), _BF16),
        grid=(b, nb),
        in_specs=[
            pl.BlockSpec(memory_space=pltpu.SMEM),
            pl.BlockSpec((None, BLOCK, aw), lambda bi, i: (bi, i, 0)),
            kv_spec(k_col, -1), kv_spec(k_col, 0), kv_spec(k_col, 1),
            kv_spec(v_col, -1), kv_spec(v_col, 0), kv_spec(v_col, 1),
            pl.BlockSpec((1, HEAD_DIM), lambda bi, i: (0, 0)),
            pl.BlockSpec((1, HEAD_DIM), lambda bi, i: (0, 0)),
        ],
        out_specs=pl.BlockSpec((None, BLOCK, aw), lambda bi, i: (bi, i, 0)),
        compiler_params=_params(("arbitrary", "arbitrary")),
        name="attn",
    )(sink.astype(_F32), proj3, proj3, proj3, proj3, proj3, proj3, proj3,
      q_gain.reshape(1, HEAD_DIM).astype(_F32), k_gain.reshape(1, HEAD_DIM).astype(_F32))


def _ssm_matrices(a_re, a_im, log_step, b_re, b_im, c_re, c_im, d_skip):
    hp = lax.Precision.HIGHEST
    n_l, n_c, n_s = SSM_CHUNK, SSM_GROUP, SSM_STATE
    n_g = a_re.shape[1]
    a_re, a_im, b_re, b_im, c_re, c_im = (v.astype(_F32) for v in (a_re, a_im, b_re, b_im, c_re, c_im))
    step = jnp.exp(log_step.astype(_F32))[..., None]
    steps = jnp.arange(n_l + 1, dtype=_F32)[:, None, None, None]
    mag = jnp.exp((a_re * step)[None] * steps)
    ang = (a_im * step)[None] * steps
    p_re, p_im = mag * jnp.cos(ang), mag * jnp.sin(ang)
    num_re, num_im = p_re[1] - 1.0, p_im[1]
    den = a_re * a_re + a_im * a_im
    f_re = ((num_re * a_re + num_im * a_im) / den)[..., None]
    f_im = ((num_im * a_re - num_re * a_im) / den)[..., None]
    bb_re, bb_im = f_re * b_re - f_im * b_im, f_re * b_im + f_im * b_re
    pe_re, pe_im = p_re[:, :, :, None, :], p_im[:, :, :, None, :]
    cp_re = c_re[None] * pe_re - c_im[None] * pe_im
    cp_im = c_re[None] * pe_im + c_im[None] * pe_re

    kern = (jnp.einsum('tdgcn,dgne->dtgce', cp_re[:n_l], bb_re, precision=hp)
            - jnp.einsum('tdgcn,dgne->dtgce', cp_im[:n_l], bb_im, precision=hp))
    s_idx = jnp.arange(n_l)[:, None]
    t_idx = jnp.arange(n_l)[None, :]
    tau_f, tau_b = t_idx - s_idx, s_idx - t_idx
    resp = (jnp.where((tau_f >= 0)[:, :, None, None, None], kern[0][jnp.clip(tau_f, 0, n_l - 1)], 0.0)
            + jnp.where((tau_b >= 0)[:, :, None, None, None], kern[1][jnp.clip(tau_b, 0, n_l - 1)], 0.0))
    skip = (jnp.eye(n_l, dtype=_F32)[:, :, None, None, None]
            * (jnp.eye(n_c, dtype=_F32)[None, None, None] * d_skip.astype(_F32).reshape(1, 1, n_g, n_c, 1)))
    m_g = (resp + skip).transpose(2, 0, 4, 1, 3)

    def state_in(direction, powers):
        q_re, q_im = p_re[powers, direction][:, :, None, :], p_im[powers, direction][:, :, None, :]
        t_re, t_im = bb_re[direction].transpose(0, 2, 1)[None], bb_im[direction].transpose(0, 2, 1)[None]
        return q_re * t_re - q_im * t_im, q_re * t_im + q_im * t_re

    w_in_g = jnp.stack(state_in(0, n_l - 1 - jnp.arange(n_l)) + state_in(1, jnp.arange(n_l)), axis=3)
    w_in_g = w_in_g.transpose(1, 0, 2, 3, 4)
    out_f, out_b = 1 + jnp.arange(n_l), n_l - jnp.arange(n_l)
    w_out_g = jnp.stack([z.transpose(1, 3, 0, 2) for z in
                         (cp_re[out_f, 0], -cp_im[out_f, 0], cp_re[out_b, 1], -cp_im[out_b, 1])], axis=1)
    lam_g = jnp.stack([p_re[n_l, 0], p_im[n_l, 0], p_re[n_l, 1], p_im[n_l, 1]], axis=1)

    gp = LANES // n_c
    n_sg = n_g // gp

    def per_tile(z):
        _, a, b, c, d = z.shape
        return z.reshape(n_sg, gp, a, b, c, d).transpose(0, 2, 3, 4, 1, 5).reshape(n_sg, a * b, c * gp * d)

    m, w_in, w_out = per_tile(m_g), per_tile(w_in_g), per_tile(w_out_g)
    lam = lam_g.reshape(n_sg, gp, 4, n_s).transpose(0, 2, 1, 3).reshape(n_sg, 4, gp * n_s)
    return m.astype(_BF16), w_in.astype(_BF16), w_out.astype(_BF16), lam


def _expand_block_diagonal(compact_ref, full_scr, block_rows, col_group):
    gp = LANES // SSM_GROUP
    width = compact_ref.shape[1]
    group_of_col = (lax.broadcasted_iota(jnp.int32, (block_rows, width), 1) // col_group) % gp
    zero = jnp.zeros((block_rows, width), compact_ref.dtype)
    for a in range(compact_ref.shape[0] // block_rows):
        block = compact_ref[a * block_rows:(a + 1) * block_rows, :]
        for g in range(gp):
            full_scr[(a * gp + g) * block_rows:(a * gp + g + 1) * block_rows, :] = jnp.where(group_of_col == g, block, zero)


def _ssm_body(x_ref, m_ref, win_ref, wout_ref, lam_ref, o_ref, lhs_scr, st_scr, io_scr, m_scr, win_scr, wout_scr,
              *, n_chunks, n_batch):
    n_l = SSM_CHUNK
    rows = n_batch * n_chunks
    pairs = n_l // 2
    hi = jnp.uint32(0xFFFF0000)

    @pl.when(pl.program_id(1) == 0)
    def _():
        _expand_block_diagonal(m_ref, m_scr, SSM_GROUP, SSM_GROUP)
        _expand_block_diagonal(win_ref, win_scr, SSM_GROUP, SSM_STATE)
        _expand_block_diagonal(wout_ref, wout_scr, SSM_STATE, SSM_GROUP)

    io_scr[...] = pltpu.bitcast(x_ref[...], jnp.uint32)
    for jj in range(pairs):
        w = io_scr[pl.ds(jj, rows, stride=pairs), :]
        even = pltpu.bitcast(w << 16, _F32)
        odd = pltpu.bitcast(w & hi, _F32)
        lhs_scr[:, (2 * jj) * LANES:(2 * jj + 1) * LANES] = even.astype(_BF16)
        lhs_scr[:, (2 * jj + 1) * LANES:(2 * jj + 2) * LANES] = odd.astype(_BF16)

    inc = jnp.dot(lhs_scr[...], win_scr[...], preferred_element_type=_F32)
    n_tiles = st_scr.shape[0]
    q = n_tiles // 4
    for t in range(n_tiles):
        st_scr[t] = inc[:, t * LANES:(t + 1) * LANES]
    lam = lam_ref[...]
    coef = [[jnp.broadcast_to(lam[p:p + 1, i * LANES:(i + 1) * LANES], (n_batch, LANES)) for i in range(q)]
            for p in range(4)]

    def step(k, carry):
        rf = pl.ds(k, n_batch, stride=n_chunks)
        rb = pl.ds(n_chunks - 1 - k, n_batch, stride=n_chunks)
        new = list(carry)
        for i in range(q):
            for base, r in ((0, rf), (2 * q, rb)):
                t_re, t_im = base + i, base + q + i
                a_re, a_im = coef[base // q][i], coef[base // q + 1][i]
                s_re, s_im = carry[t_re], carry[t_im]
                d_re, d_im = st_scr[t_re, r, :], st_scr[t_im, r, :]
                st_scr[t_re, r, :] = s_re
                st_scr[t_im, r, :] = s_im
                new[t_re] = a_re * s_re - a_im * s_im + d_re
                new[t_im] = a_re * s_im + a_im * s_re + d_im
        return tuple(new)

    zero = jnp.zeros((n_batch, LANES), _F32)
    lax.fori_loop(0, n_chunks, step, (zero,) * n_tiles, unroll=True)
    states = jnp.concatenate([st_scr[t] for t in range(n_tiles)], axis=1).astype(_BF16)
    y = (jnp.dot(lhs_scr[...], m_scr[...], preferred_element_type=_F32)
         + jnp.dot(states, wout_scr[...], preferred_element_type=_F32))
    y = _gelu(y).astype(_BF16).astype(_F32)

    for jj in range(pairs):
        even = pltpu.bitcast(y[:, (2 * jj) * LANES:(2 * jj + 1) * LANES], jnp.uint32)
        odd = pltpu.bitcast(y[:, (2 * jj + 1) * LANES:(2 * jj + 2) * LANES], jnp.uint32)
        io_scr[pl.ds(jj, rows, stride=pairs), :] = (odd & hi) | (even >> 16)
    o_ref[...] = pltpu.bitcast(io_scr[...], _BF16)


def _ssm(proj, u_col, m, w_in, w_out, lam, n_batch_total, seq_len):
    t = proj.shape[0]
    n_sg, _, width = m.shape
    bt = 4
    n_chunks = seq_len // SSM_CHUNK
    rows = bt * n_chunks
    n_state = w_in.shape[2]
    n_tiles = n_state // LANES
    col0 = u_col // LANES
    compact = lambda arr: pl.BlockSpec((None,) + arr.shape[1:], lambda s, i: (s, 0, 0),
                                       pipeline_mode=pl.Buffered(1))
    return pl.pallas_call(
        functools.partial(_ssm_body, n_chunks=n_chunks, n_batch=bt),
        out_shape=jax.ShapeDtypeStruct((t, n_sg * LANES), _BF16),
        grid=(n_sg, n_batch_total // bt),
        in_specs=[
            pl.BlockSpec((bt * seq_len, LANES), lambda s, i: (i, col0 + s)),
            compact(m), compact(w_in), compact(w_out), compact(lam),
        ],
        out_specs=pl.BlockSpec((bt * seq_len, LANES), lambda s, i: (i, s)),
        scratch_shapes=[pltpu.VMEM((rows, width), _BF16), pltpu.VMEM((n_tiles, rows, LANES), _F32),
                        pltpu.VMEM((bt * seq_len // 2, LANES), jnp.uint32),
                        pltpu.VMEM((width, width), _BF16), pltpu.VMEM((width, n_state), _BF16),
                        pltpu.VMEM((n_state, width), _BF16)],
        compiler_params=_params(("arbitrary", "arbitrary")),
        name="ssm",
    )(proj, m, w_in, w_out, lam)


def _merge_body(attn_ref, yact_ref, ga_ref, gb_ref, wo_ref, wa_ref, wb_ref, o_ref):
    y_a = jnp.dot(attn_ref[...], wo_ref[...], preferred_element_type=_F32)
    yact = yact_ref[...]
    glu = (jnp.dot(yact, wa_ref[...], preferred_element_type=_F32)
           * jax.nn.sigmoid(jnp.dot(yact, wb_ref[...], preferred_element_type=_F32)))
    o_ref[...] = (jax.nn.sigmoid(ga_ref[...].astype(_F32)) * y_a
                  + jax.nn.sigmoid(gb_ref[...].astype(_F32)) * glu).astype(o_ref.dtype)


def _merge(attn2, yact2, proj, gate_a_col, gate_b_col, wo, wa, wb):
    t, aw = attn2.shape
    sw = yact2.shape[1]
    d = wo.shape[1]
    tm, tn = min(512, t), d
    ga0, gb0 = gate_a_col // tn, gate_b_col // tn
    once = pl.Buffered(1)
    return pl.pallas_call(
        _merge_body,
        out_shape=jax.ShapeDtypeStruct((t, d), _BF16),
        grid=(t // tm, d // tn),
        in_specs=[
            pl.BlockSpec((tm, aw), lambda i, j: (i, 0)),
            pl.BlockSpec((tm, sw), lambda i, j: (i, 0)),
            pl.BlockSpec((tm, tn), lambda i, j: (i, ga0 + j)),
            pl.BlockSpec((tm, tn), lambda i, j: (i, gb0 + j)),
            pl.BlockSpec((aw, tn), lambda i, j: (0, j), pipeline_mode=once),
            pl.BlockSpec((sw, tn), lambda i, j: (0, j), pipeline_mode=once),
            pl.BlockSpec((sw, tn), lambda i, j: (0, j), pipeline_mode=once),
        ],
        out_specs=pl.BlockSpec((tm, tn), lambda i, j: (i, j)),
        compiler_params=_params(("arbitrary", "arbitrary")),
        name="merge",
    )(attn2, yact2, proj, proj, wo, wa, wb)


def _outproj_body(x_ref, m_ref, w_ref, g_ref, wq_ref, x1_ref, h2_ref, q_ref):
    x1 = x_ref[...] + jnp.dot(m_ref[...], w_ref[...], preferred_element_type=_F32)
    x1_ref[...] = x1
    h2 = (x1 * _rms_scale(x1) * g_ref[...]).astype(h2_ref.dtype)
    h2_ref[...] = h2
    q_ref[...] = jnp.dot(h2, wq_ref[...], preferred_element_type=_F32).astype(q_ref.dtype)


def _outproj(x2, merged, w_out, gain, wq):
    t, d = x2.shape
    qw = wq.shape[1]
    tm = min(512, t)
    rows = lambda width: pl.BlockSpec((tm, width), lambda i: (i, 0))
    whole = lambda arr: pl.BlockSpec(arr.shape, lambda i: (0, 0))
    return pl.pallas_call(
        _outproj_body,
        out_shape=(jax.ShapeDtypeStruct((t, d), _F32), jax.ShapeDtypeStruct((t, d), _BF16),
                   jax.ShapeDtypeStruct((t, qw), _BF16)),
        grid=(t // tm,),
        in_specs=[rows(d), rows(d), whole(w_out), pl.BlockSpec((1, d), lambda i: (0, 0)), whole(wq)],
        out_specs=(rows(d), rows(d), rows(qw)),
        compiler_params=_params(("arbitrary",)),
        name="outproj",
    )(x2, merged, w_out, gain.reshape(1, d), wq)


_CAND_WIDE = 4
_CAND_TAIL = tuple((p1, p2) for p1 in range(_CAND_WIDE, PEER_TOPK) for p2 in range(PEER_TOPK)
                   if (p1 + 1) * (p2 + 1) <= PEER_TOPK)
_SUBLANES = 8


def _top_rows(vals, k, exact, want_rank):
    n = vals.shape[0]
    row = lax.broadcasted_iota(jnp.int32, vals.shape, 0).astype(_F32) if exact else None
    rank = jnp.full(vals.shape, float(k), _F32) if want_rank else None
    work = vals
    tops = []
    for r in range(k):
        best = jnp.max(work, axis=0, keepdims=True)
        hit = work == best
        if exact:
            first = jnp.min(jnp.where(hit, row, float(n)), axis=0, keepdims=True)
            hit = row == first
        if want_rank:
            rank = jnp.where(hit, float(r), rank)
        work = jnp.where(hit, -jnp.inf, work)
        tops.append(best)
    return jnp.concatenate(tops, axis=0), rank, work


def _extracted(left):
    return jnp.sum(jnp.where(left == -jnp.inf, 1.0, 0.0), axis=0, keepdims=True)


def _rank_keys(s1, s2, exact):
    k = PEER_TOPK
    top1, rank1, left1 = _top_rows(s1, k, exact, want_rank=exact)
    top2, rank2, left2 = _top_rows(s2, k, exact, want_rank=True)
    e2 = jnp.exp(s2 - top2[0:1])
    bad = None if exact else jnp.abs(_extracted(left1) - k) + jnp.abs(_extracted(left2) - k)
    return top1, top2, rank1, rank2, e2, bad


def _rank_pairs(s1, top1, top2, rank1, exact):
    k = PEER_TOPK
    n_pad = -len(_CAND_TAIL) % _SUBLANES
    pad = [jnp.full_like(top1[0:1], -jnp.inf)] * n_pad
    cand = jnp.concatenate([top1[p:p + 1] + top2 for p in range(_CAND_WIDE)]
                           + [top1[p1:p1 + 1] + top2[p2:p2 + 1] for p1, p2 in _CAND_TAIL] + pad, axis=0)
    top_c, _, left_c = _top_rows(cand, k, exact, want_rank=False)
    chosen = jnp.where(left_c == -jnp.inf, 1.0, 0.0)
    per_rank = [jnp.sum(chosen[p * k:(p + 1) * k], axis=0, keepdims=True) for p in range(_CAND_WIDE)]
    for p1 in range(_CAND_WIDE, k):
        rows = [_CAND_WIDE * k + i for i, (a, _) in enumerate(_CAND_TAIL) if a == p1]
        per_rank.append(jnp.sum(chosen[rows[0]:rows[-1] + 1], axis=0, keepdims=True))
    count = jnp.zeros(s1.shape, _F32)
    for p in range(k):
        count = jnp.where((rank1 == p) if exact else (s1 == top1[p:p + 1]), per_rank[p], count)
    z = jnp.sum(jnp.exp(top_c - top_c[0:1]), axis=0, keepdims=True)
    coef = jnp.exp(s1 - top1[0:1]) / z
    bad = None if exact else jnp.abs(_extracted(left_c) - (k + n_pad))
    return count, coef, bad


def _select_body(q_ref, k1_ref, k2_ref, cnt_ref, coef_ref, rank_ref, e2_ref, s1_scr, s2_scr):
    q = q_ref[...]
    s1_scr[...] = lax.dot_general(k1_ref[...], q[:, :PEER_HALF], _NT, preferred_element_type=_F32)
    s2_scr[...] = lax.dot_general(k2_ref[...], q[:, PEER_HALF:], _NT, preferred_element_type=_F32)
    n_tiles = s1_scr.shape[1] // LANES

    def lanes_of(i):
        return pl.ds(pl.multiple_of(i * LANES, LANES), LANES)

    def keys_stage(i, exact):
        lanes = lanes_of(i)
        top1, top2, rank1, rank2, e2, bad = _rank_keys(s1_scr[:, lanes], s2_scr[:, lanes], exact)
        rank_ref[:, lanes] = rank2.astype(rank_ref.dtype)
        e2_ref[:, lanes] = e2.astype(e2_ref.dtype)
        return top1, top2, rank1, bad

    def pairs_stage(i, top1, top2, rank1, exact):
        lanes = lanes_of(i)
        count, coef, bad = _rank_pairs(s1_scr[:, lanes], top1, top2, rank1, exact)
        cnt_ref[:, lanes] = count
        coef_ref[:, lanes] = coef
        return bad

    top1, top2, _, bad = keys_stage(0, exact=False)

    def skewed(i, carry):
        top1, top2, bad = carry
        nxt1, nxt2, _, bad_keys = keys_stage(i + 1, exact=False)
        bad_pairs = pairs_stage(i, top1, top2, None, exact=False)
        return nxt1, nxt2, jnp.maximum(bad, jnp.maximum(bad_keys, bad_pairs))

    top1, top2, bad = lax.fori_loop(0, n_tiles - 1, skewed, (top1, top2, bad))
    bad = jnp.maximum(bad, pairs_stage(n_tiles - 1, top1, top2, None, exact=False))

    @pl.when(jnp.max(bad) > 0.0)
    def _():
        def exact_tile(i, carry):
            top1, top2, rank1, _ = keys_stage(i, exact=True)
            pairs_stage(i, top1, top2, rank1, exact=True)
            return carry

        lax.fori_loop(0, n_tiles, exact_tile, 0)


def _select(q, keys1, keys2):
    t = q.shape[0]
    tm = min(512, t)
    qd = 2 * PEER_HALF
    out = jax.ShapeDtypeStruct((PEER_HEADS, N_KEYS, t), _F32)
    out16 = jax.ShapeDtypeStruct((PEER_HEADS, N_KEYS, t), _BF16)
    spec = pl.BlockSpec((None, N_KEYS, tm), lambda i, h: (h, 0, i))
    return pl.pallas_call(
        _select_body,
        out_shape=(out, out, out16, out16),
        grid=(t // tm, PEER_HEADS),
        in_specs=[
            pl.BlockSpec((tm, qd), lambda i, h: (i, h)),
            pl.BlockSpec((None, N_KEYS, PEER_HALF), lambda i, h: (h, 0, 0)),
            pl.BlockSpec((None, N_KEYS, PEER_HALF), lambda i, h: (h, 0, 0)),
        ],
        out_specs=(spec, spec, spec, spec),
        scratch_shapes=[pltpu.VMEM((N_KEYS, tm), _F32), pltpu.VMEM((N_KEYS, tm), _F32)],
        compiler_params=_params(("arbitrary", "arbitrary")),
        name="select",
    )(q, keys1, keys2)


def _peer_body(h2_ref, down_ref, upt_ref, cnt_ref, coef_ref, rank_ref, e2_ref, x1_ref, o_ref,
               acc_scr, wg_scr, *, key_rows):
    j = pl.program_id(1)

    @pl.when(j == 0)
    def _():
        acc_scr[...] = jnp.zeros_like(acc_scr)

    act = lax.dot_general(down_ref[...], h2_ref[...], _NT, preferred_element_type=_F32)
    gact = _gelu(act).astype(_BF16)
    block = (N_KEYS, gact.shape[1])
    zero = jnp.zeros(block, _BF16)
    for a in range(key_rows):
        rows = slice(a * N_KEYS, (a + 1) * N_KEYS)
        w = None
        for h in range(PEER_HEADS):
            cnt = jnp.broadcast_to(cnt_ref[h, a:a + 1, :].astype(_BF16), block)
            coef = jnp.broadcast_to(coef_ref[h, a:a + 1, :].astype(_BF16), block)
            term = jnp.where(rank_ref[h] < cnt, e2_ref[h], zero) * coef
            w = term if w is None else w + term
        wg_scr[rows, :] = w * gact[rows, :]
    acc_scr[...] += jnp.dot(upt_ref[...], wg_scr[...], preferred_element_type=_F32)

    @pl.when(j == pl.num_programs(1) - 1)
    def _():
        o_ref[...] = x1_ref[...] + acc_scr[...].T


def _peer(h2, down, up_t, count, coef, rank2, e2, x1):
    t, d = h2.shape
    n_e = down.shape[0]
    tm, te = min(512, t), 1024
    key_rows = te // N_KEYS
    once = pl.Buffered(1)
    return pl.pallas_call(
        functools.partial(_peer_body, key_rows=key_rows),
        out_shape=jax.ShapeDtypeStruct((t, d), _F32),
        grid=(t // tm, n_e // te),
        in_specs=[
            pl.BlockSpec((tm, d), lambda i, j: (i, 0), pipeline_mode=once),
            pl.BlockSpec((te, d), lambda i, j: (j, 0)),
            pl.BlockSpec((d, te), lambda i, j: (0, j)),
            pl.BlockSpec((PEER_HEADS, key_rows, tm), lambda i, j: (0, j, i)),
            pl.BlockSpec((PEER_HEADS, key_rows, tm), lambda i, j: (0, j, i)),
            pl.BlockSpec((PEER_HEADS, N_KEYS, tm), lambda i, j: (0, 0, i), pipeline_mode=once),
            pl.BlockSpec((PEER_HEADS, N_KEYS, tm), lambda i, j: (0, 0, i), pipeline_mode=once),
            pl.BlockSpec((tm, d), lambda i, j: (i, 0), pipeline_mode=once),
        ],
        out_specs=pl.BlockSpec((tm, d), lambda i, j: (i, 0)),
        scratch_shapes=[pltpu.VMEM((d, tm), _F32), pltpu.VMEM((te, tm), _BF16)],
        compiler_params=_params(("arbitrary", "arbitrary")),
        name="peer",
    )(h2, down, up_t, count, coef, rank2, e2, x1)


def _layer(x, mix_g, w_in, q_g, k_g, sink, w_attn_o, a_re, a_im, log_step, b_re, b_im, c_re, c_im, d_skip,
           glu_a, glu_b, w_out, ffn_g, wq, keys1, keys2, down, up):
    b, s, d = x.shape
    t = b * s
    aw, kw = N_Q_HEADS * HEAD_DIM, N_KV_HEADS * HEAD_DIM
    sw = d_skip.shape[0]
    u_col = aw + 2 * kw
    ga_col = u_col + sw
    gb_col = ga_col + d
    x2 = x.reshape(t, d)

    proj = _inproj(x2, mix_g, w_in.astype(_BF16))
    attn = _attention(proj.reshape(b, s, -1), q_g, k_g, sink).reshape(t, aw)

    m, s_in, s_out, lam = _ssm_matrices(a_re, a_im, log_step, b_re, b_im, c_re, c_im, d_skip)
    yact = _ssm(proj, u_col, m, s_in, s_out, lam, b, s)

    merged = _merge(attn, yact, proj, ga_col, gb_col,
                    w_attn_o.astype(_BF16), glu_a.astype(_BF16), glu_b.astype(_BF16))
    x1, h2, q = _outproj(x2, merged, w_out.astype(_BF16), ffn_g, wq.astype(_BF16))

    count, coef, rank2, e2 = _select(q, keys1.astype(_BF16), keys2.astype(_BF16))
    out = _peer(h2, down.astype(_BF16), up.T.astype(_BF16), count, coef, rank2, e2, x1)
    return out.reshape(b, s, d)


def kernel(x, mix_norm_g, w_in, q_norm_g, k_norm_g, attn_sink, w_attn_o, ssm_a_re, ssm_a_im, ssm_log_step,
           ssm_b_re, ssm_b_im, ssm_c_re, ssm_c_im, ssm_d, glu_w_a, glu_w_b, w_out, ffn_norm_g, peer_w_query,
           peer_sub_keys_1, peer_sub_keys_2, peer_down, peer_up):
    for l in range(mix_norm_g.shape[0]):
        x = _layer(x, mix_norm_g[l], w_in[l], q_norm_g[l], k_norm_g[l], attn_sink[l], w_attn_o[l],
                   ssm_a_re[l], ssm_a_im[l], ssm_log_step[l], ssm_b_re[l], ssm_b_im[l], ssm_c_re[l],
                   ssm_c_im[l], ssm_d[l], glu_w_a[l], glu_w_b[l], w_out[l], ffn_norm_g[l], peer_w_query[l],
                   peer_sub_keys_1[l], peer_sub_keys_2[l], peer_down[l], peer_up[l])
    return x
```

```python
import functools
import math

import jax
import jax.numpy as jnp
from jax import lax
from jax.experimental import pallas as pl
from jax.experimental.pallas import tpu as pltpu

N_Q_HEADS = 16
N_KV_HEADS = 4
Q_GROUP = N_Q_HEADS // N_KV_HEADS
HEAD_DIM = 128
WINDOW = 128
BLOCK = 128
SSM_GROUP = 16
SSM_STATE = 64
SSM_CHUNK = 16
PEER_HEADS = 8
N_KEYS = 128
PEER_HALF = 128
PEER_TOPK = 16
RMS_EPS = 1e-6
LANES = 128
VMEM_LIMIT = 56 * 1024 * 1024

_F32 = jnp.float32
_BF16 = jnp.bfloat16
_NT = (((1,), (1,)), ((), ()))


def _params(semantics):
    return pltpu.CompilerParams(dimension_semantics=semantics, vmem_limit_bytes=VMEM_LIMIT)


def _gelu(v):
    return 0.5 * v * (1.0 + lax.erf(v * (1.0 / math.sqrt(2.0))))


def _rms_scale(v):
    return lax.rsqrt(jnp.mean(v * v, axis=-1, keepdims=True) + RMS_EPS)


def _inproj_body(x_ref, g_ref, w_ref, o_ref, h_scr):
    @pl.when(pl.program_id(1) == 0)
    def _():
        x = x_ref[...]
        h_scr[...] = (x * _rms_scale(x) * g_ref[...]).astype(h_scr.dtype)

    o_ref[...] = jnp.dot(h_scr[...], w_ref[...], preferred_element_type=_F32).astype(o_ref.dtype)


def _inproj(x2, gain, w_bf16):
    t, d = x2.shape
    n = w_bf16.shape[1]
    tm, tn = min(1024, t), 1024
    return pl.pallas_call(
        _inproj_body,
        out_shape=jax.ShapeDtypeStruct((t, n), _BF16),
        grid=(t // tm, n // tn),
        in_specs=[
            pl.BlockSpec((tm, d), lambda i, j: (i, 0)),
            pl.BlockSpec((1, d), lambda i, j: (0, 0)),
            pl.BlockSpec((d, tn), lambda i, j: (0, j)),
        ],
        out_specs=pl.BlockSpec((tm, tn), lambda i, j: (i, j)),
        scratch_shapes=[pltpu.VMEM((tm, d), _BF16)],
        compiler_params=_params(("arbitrary", "arbitrary")),
        name="inproj",
    )(x2, gain.reshape(1, d), w_bf16)


def _attn_body(sink_ref, q_ref, kp_ref, kc_ref, kn_ref, vp_ref, vc_ref, vn_ref, qg_ref, kg_ref, o_ref,
               *, seq_len, slopes):
    i = pl.program_id(1)
    row = lax.broadcasted_iota(jnp.int32, (BLOCK, 3 * BLOCK), 0)
    col = lax.broadcasted_iota(jnp.int32, (BLOCK, 3 * BLOCK), 1)
    dist = jnp.abs(row + BLOCK - col)
    kpos = (i - 1) * BLOCK + col
    valid = (dist <= WINDOW) & (kpos >= 0) & (kpos < seq_len)
    neg_dist = jnp.where(valid, -dist.astype(_F32), -jnp.inf)

    k3 = jnp.concatenate([kp_ref[...], kc_ref[...], kn_ref[...]], axis=0).astype(_F32)
    v3 = jnp.concatenate([vp_ref[...], vc_ref[...], vn_ref[...]], axis=0)
    qf = q_ref[...].astype(_F32)
    q_gain = qg_ref[...] * (HEAD_DIM ** -0.5)
    k_gain = kg_ref[...]

    for kv in range(N_KV_HEADS):
        kh = k3[:, kv * HEAD_DIM:(kv + 1) * HEAD_DIM]
        kh = (kh * _rms_scale(kh) * k_gain).astype(_BF16)
        q_rows = []
        for g in range(Q_GROUP):
            h = kv * Q_GROUP + g
            qh = qf[:, h * HEAD_DIM:(h + 1) * HEAD_DIM]
            q_rows.append((qh * _rms_scale(qh) * q_gain).astype(_BF16))
        scores = lax.dot_general(jnp.concatenate(q_rows, axis=0), kh, _NT,
                                 preferred_element_type=_F32)
        probs, denoms = [], []
        for g in range(Q_GROUP):
            h = kv * Q_GROUP + g
            s = scores[g * BLOCK:(g + 1) * BLOCK] + slopes[h] * neg_dist
            sink = sink_ref[h]
            m = jnp.maximum(jnp.max(s, axis=-1, keepdims=True), sink)
            p = jnp.exp(s - m)
            denoms.append(jnp.sum(p, axis=-1, keepdims=True) + jnp.exp(sink - m))
            probs.append(p.astype(_BF16))
        pv = jnp.dot(jnp.concatenate(probs, axis=0), v3[:, kv * HEAD_DIM:(kv + 1) * HEAD_DIM],
                     preferred_element_type=_F32)
        for g in range(Q_GROUP):
            h = kv * Q_GROUP + g
            o_ref[:, h * HEAD_DIM:(h + 1) * HEAD_DIM] = (
                pv[g * BLOCK:(g + 1) * BLOCK] / denoms[g]).astype(o_ref.dtype)


def _attention(proj3, q_gain, k_gain, sink):
    b, s, _ = proj3.shape
    nb = s // BLOCK
    aw, kw = N_Q_HEADS * HEAD_DIM, N_KV_HEADS * HEAD_DIM
    k_col, v_col = aw // kw, aw // kw + 1
    slopes = tuple(2.0 ** (-8.0 * (h + 1.0) / N_Q_HEADS) for h in range(N_Q_HEADS))

    def kv_spec(col, shift):
        return pl.BlockSpec((None, BLOCK, kw),
                            lambda bi, i: (bi, jnp.clip(i + shift, 0, nb - 1), col))

    return pl.pallas_call(
        functools.partial(_attn_body, seq_len=s, slopes=slopes),
        out_shape=jax.ShapeDtypeStruct((b, s, aw), _BF16),
        grid=(b, nb),
        in_specs=[
            pl.BlockSpec(memory_space=pltpu.SMEM),
            pl.BlockSpec((None, BLOCK, aw), lambda bi, i: (bi, i, 0)),
            kv_spec(k_col, -1), kv_spec(k_col, 0), kv_spec(k_col, 1),
            kv_spec(v_col, -1), kv_spec(v_col, 0), kv_spec(v_col, 1),
            pl.BlockSpec((1, HEAD_DIM), lambda bi, i: (0, 0)),
            pl.BlockSpec((1, HEAD_DIM), lambda bi, i: (0, 0)),
        ],
        out_specs=pl.BlockSpec((None, BLOCK, aw), lambda bi, i: (bi, i, 0)),
        compiler_params=_params(("arbitrary", "arbitrary")),
        name="attn",
    )(sink.astype(_F32), proj3, proj3, proj3, proj3, proj3, proj3, proj3,
      q_gain.reshape(1, HEAD_DIM).astype(_F32), k_gain.reshape(1, HEAD_DIM).astype(_F32))


def _ssm_matrices(a_re, a_im, log_step, b_re, b_im, c_re, c_im, d_skip):
    hp = lax.Precision.HIGHEST
    n_l, n_c, n_s = SSM_CHUNK, SSM_GROUP, SSM_STATE
    n_g = a_re.shape[1]
    a_re, a_im, b_re, b_im, c_re, c_im = (v.astype(_F32) for v in (a_re, a_im, b_re, b_im, c_re, c_im))
    step = jnp.exp(log_step.astype(_F32))[..., None]
    steps = jnp.arange(n_l + 1, dtype=_F32)[:, None, None, None]
    mag = jnp.exp((a_re * step)[None] * steps)
    ang = (a_im * step)[None] * steps
    p_re, p_im = mag * jnp.cos(ang), mag * jnp.sin(ang)
    num_re, num_im = p_re[1] - 1.0, p_im[1]
    den = a_re * a_re + a_im * a_im
    f_re = ((num_re * a_re + num_im * a_im) / den)[..., None]
    f_im = ((num_im * a_re - num_re * a_im) / den)[..., None]
    bb_re, bb_im = f_re * b_re - f_im * b_im, f_re * b_im + f_im * b_re
    pe_re, pe_im = p_re[:, :, :, None, :], p_im[:, :, :, None, :]
    cp_re = c_re[None] * pe_re - c_im[None] * pe_im
    cp_im = c_re[None] * pe_im + c_im[None] * pe_re

    kern = (jnp.einsum('tdgcn,dgne->dtgce', cp_re[:n_l], bb_re, precision=hp)
            - jnp.einsum('tdgcn,dgne->dtgce', cp_im[:n_l], bb_im, precision=hp))
    gp = LANES // n_c
    n_sg = n_g // gp
    skip = jnp.eye(n_c, dtype=_F32)[None] * d_skip.astype(_F32).reshape(n_g, n_c, 1)
    lagged = jnp.concatenate([kern[1][:0:-1], (kern[0][0] + kern[1][0] + skip)[None], kern[0][1:]], axis=0)
    lag = jnp.arange(n_l)[None, :] - jnp.arange(n_l)[:, None] + (n_l - 1)
    select_lag = (lag[:, :, None] == jnp.arange(2 * n_l - 1)[None, None, :]).astype(_F32)
    m = jnp.einsum('jJt,tshCc->sjcJhC', select_lag, lagged.reshape(2 * n_l - 1, n_sg, gp, n_c, n_c))
    m = m.reshape(n_sg, n_l * n_c, n_l * gp * n_c)

    def state_in(direction, powers):
        q_re, q_im = p_re[powers, direction][:, :, None, :], p_im[powers, direction][:, :, None, :]
        t_re, t_im = bb_re[direction].transpose(0, 2, 1)[None], bb_im[direction].transpose(0, 2, 1)[None]
        return q_re * t_re - q_im * t_im, q_re * t_im + q_im * t_re

    w_in_g = jnp.stack(state_in(0, n_l - 1 - jnp.arange(n_l)) + state_in(1, jnp.arange(n_l)), axis=3)
    w_in_g = w_in_g.transpose(1, 0, 2, 3, 4)
    out_f, out_b = 1 + jnp.arange(n_l), n_l - jnp.arange(n_l)
    w_out_g = jnp.stack([z.transpose(1, 3, 0, 2) for z in
                         (cp_re[out_f, 0], -cp_im[out_f, 0], cp_re[out_b, 1], -cp_im[out_b, 1])], axis=1)
    lam_g = jnp.stack([p_re[n_l, 0], p_im[n_l, 0], p_re[n_l, 1], p_im[n_l, 1]], axis=1)

    def per_tile(z):
        _, a, b, c, d = z.shape
        return z.reshape(n_sg, gp, a, b, c, d).transpose(0, 2, 3, 4, 1, 5).reshape(n_sg, a * b, c * gp * d)

    w_in, w_out = per_tile(w_in_g), per_tile(w_out_g)
    lam = lam_g.reshape(n_sg, gp, 4, n_s).transpose(0, 2, 1, 3).reshape(n_sg, 4, gp * n_s)
    return m.astype(_BF16), w_in.astype(_BF16), w_out.astype(_BF16), lam


def _expand_block_diagonal(compact_ref, full_scr, block_rows, col_group):
    gp = LANES // SSM_GROUP
    width = compact_ref.shape[1]
    group_of_col = (lax.broadcasted_iota(jnp.int32, (block_rows, width), 1) // col_group) % gp
    zero = jnp.zeros((block_rows, width), compact_ref.dtype)
    for a in range(compact_ref.shape[0] // block_rows):
        block = compact_ref[a * block_rows:(a + 1) * block_rows, :]
        for g in range(gp):
            full_scr[(a * gp + g) * block_rows:(a * gp + g + 1) * block_rows, :] = jnp.where(group_of_col == g, block, zero)


def _ssm_body(x_ref, m_ref, win_ref, wout_ref, lam_ref, o_ref, lhs_scr, st_scr, io_scr, m_scr, win_scr, wout_scr,
              *, n_chunks, n_batch):
    n_l = SSM_CHUNK
    rows = n_batch * n_chunks
    pairs = n_l // 2
    hi = jnp.uint32(0xFFFF0000)

    @pl.when(pl.program_id(1) == 0)
    def _():
        _expand_block_diagonal(m_ref, m_scr, SSM_GROUP, SSM_GROUP)
        _expand_block_diagonal(win_ref, win_scr, SSM_GROUP, SSM_STATE)
        _expand_block_diagonal(wout_ref, wout_scr, SSM_STATE, SSM_GROUP)

    io_scr[...] = pltpu.bitcast(x_ref[...], jnp.uint32)
    for jj in range(pairs):
        w = io_scr[pl.ds(jj, rows, stride=pairs), :]
        even = pltpu.bitcast(w << 16, _F32)
        odd = pltpu.bitcast(w & hi, _F32)
        lhs_scr[:, (2 * jj) * LANES:(2 * jj + 1) * LANES] = even.astype(_BF16)
        lhs_scr[:, (2 * jj + 1) * LANES:(2 * jj + 2) * LANES] = odd.astype(_BF16)

    inc = jnp.dot(lhs_scr[...], win_scr[...], preferred_element_type=_F32)
    n_tiles = st_scr.shape[0]
    q = n_tiles // 4
    for t in range(n_tiles):
        st_scr[t] = inc[:, t * LANES:(t + 1) * LANES]
    lam = lam_ref[...]
    coef = [[jnp.broadcast_to(lam[p:p + 1, i * LANES:(i + 1) * LANES], (n_batch, LANES)) for i in range(q)]
            for p in range(4)]

    def step(k, carry):
        rf = pl.ds(k, n_batch, stride=n_chunks)
        rb = pl.ds(n_chunks - 1 - k, n_batch, stride=n_chunks)
        new = list(carry)
        for i in range(q):
            for base, r in ((0, rf), (2 * q, rb)):
                t_re, t_im = base + i, base + q + i
                a_re, a_im = coef[base // q][i], coef[base // q + 1][i]
                s_re, s_im = carry[t_re], carry[t_im]
                d_re, d_im = st_scr[t_re, r, :], st_scr[t_im, r, :]
                st_scr[t_re, r, :] = s_re
                st_scr[t_im, r, :] = s_im
                new[t_re] = a_re * s_re - a_im * s_im + d_re
                new[t_im] = a_re * s_im + a_im * s_re + d_im
        return tuple(new)

    zero = jnp.zeros((n_batch, LANES), _F32)
    lax.fori_loop(0, n_chunks, step, (zero,) * n_tiles, unroll=True)
    states = jnp.concatenate([st_scr[t] for t in range(n_tiles)], axis=1).astype(_BF16)
    y = (jnp.dot(lhs_scr[...], m_scr[...], preferred_element_type=_F32)
         + jnp.dot(states, wout_scr[...], preferred_element_type=_F32))
    y = _gelu(y).astype(_BF16).astype(_F32)

    for jj in range(pairs):
        even = pltpu.bitcast(y[:, (2 * jj) * LANES:(2 * jj + 1) * LANES], jnp.uint32)
        odd = pltpu.bitcast(y[:, (2 * jj + 1) * LANES:(2 * jj + 2) * LANES], jnp.uint32)
        io_scr[pl.ds(jj, rows, stride=pairs), :] = (odd & hi) | (even >> 16)
    o_ref[...] = pltpu.bitcast(io_scr[...], _BF16)


def _ssm(proj, u_col, m, w_in, w_out, lam, n_batch_total, seq_len):
    t = proj.shape[0]
    n_sg, _, width = m.shape
    bt = 4
    n_chunks = seq_len // SSM_CHUNK
    rows = bt * n_chunks
    n_state = w_in.shape[2]
    n_tiles = n_state // LANES
    col0 = u_col // LANES
    compact = lambda arr: pl.BlockSpec((None,) + arr.shape[1:], lambda s, i: (s, 0, 0),
                                       pipeline_mode=pl.Buffered(1))
    return pl.pallas_call(
        functools.partial(_ssm_body, n_chunks=n_chunks, n_batch=bt),
        out_shape=jax.ShapeDtypeStruct((t, n_sg * LANES), _BF16),
        grid=(n_sg, n_batch_total // bt),
        in_specs=[
            pl.BlockSpec((bt * seq_len, LANES), lambda s, i: (i, col0 + s)),
            compact(m), compact(w_in), compact(w_out), compact(lam),
        ],
        out_specs=pl.BlockSpec((bt * seq_len, LANES), lambda s, i: (i, s)),
        scratch_shapes=[pltpu.VMEM((rows, width), _BF16), pltpu.VMEM((n_tiles, rows, LANES), _F32),
                        pltpu.VMEM((bt * seq_len // 2, LANES), jnp.uint32),
                        pltpu.VMEM((width, width), _BF16), pltpu.VMEM((width, n_state), _BF16),
                        pltpu.VMEM((n_state, width), _BF16)],
        compiler_params=_params(("arbitrary", "arbitrary")),
        name="ssm",
    )(proj, m, w_in, w_out, lam)


def _merge_body(attn_ref, yact_ref, ga_ref, gb_ref, wo_ref, wa_ref, wb_ref, o_ref):
    y_a = jnp.dot(attn_ref[...], wo_ref[...], preferred_element_type=_F32)
    yact = yact_ref[...]
    glu = (jnp.dot(yact, wa_ref[...], preferred_element_type=_F32)
           * jax.nn.sigmoid(jnp.dot(yact, wb_ref[...], preferred_element_type=_F32)))
    o_ref[...] = (jax.nn.sigmoid(ga_ref[...].astype(_F32)) * y_a
                  + jax.nn.sigmoid(gb_ref[...].astype(_F32)) * glu).astype(o_ref.dtype)


def _merge(attn2, yact2, proj, gate_a_col, gate_b_col, wo, wa, wb):
    t, aw = attn2.shape
    sw = yact2.shape[1]
    d = wo.shape[1]
    tm, tn = min(512, t), d
    ga0, gb0 = gate_a_col // tn, gate_b_col // tn
    once = pl.Buffered(1)
    return pl.pallas_call(
        _merge_body,
        out_shape=jax.ShapeDtypeStruct((t, d), _BF16),
        grid=(t // tm, d // tn),
        in_specs=[
            pl.BlockSpec((tm, aw), lambda i, j: (i, 0)),
            pl.BlockSpec((tm, sw), lambda i, j: (i, 0)),
            pl.BlockSpec((tm, tn), lambda i, j: (i, ga0 + j)),
            pl.BlockSpec((tm, tn), lambda i, j: (i, gb0 + j)),
            pl.BlockSpec((aw, tn), lambda i, j: (0, j), pipeline_mode=once),
            pl.BlockSpec((sw, tn), lambda i, j: (0, j), pipeline_mode=once),
            pl.BlockSpec((sw, tn), lambda i, j: (0, j), pipeline_mode=once),
        ],
        out_specs=pl.BlockSpec((tm, tn), lambda i, j: (i, j)),
        compiler_params=_params(("arbitrary", "arbitrary")),
        name="merge",
    )(attn2, yact2, proj, proj, wo, wa, wb)


def _outproj_body(x_ref, m_ref, w_ref, g_ref, wq_ref, x1_ref, h2_ref, q_ref):
    x1 = x_ref[...] + jnp.dot(m_ref[...], w_ref[...], preferred_element_type=_F32)
    x1_ref[...] = x1
    h2 = (x1 * _rms_scale(x1) * g_ref[...]).astype(h2_ref.dtype)
    h2_ref[...] = h2
    q_ref[...] = jnp.dot(h2, wq_ref[...], preferred_element_type=_F32).astype(q_ref.dtype)


def _outproj(x2, merged, w_out, gain, wq):
    t, d = x2.shape
    qw = wq.shape[1]
    tm = min(512, t)
    rows = lambda width: pl.BlockSpec((tm, width), lambda i: (i, 0))
    whole = lambda arr: pl.BlockSpec(arr.shape, lambda i: (0, 0))
    return pl.pallas_call(
        _outproj_body,
        out_shape=(jax.ShapeDtypeStruct((t, d), _F32), jax.ShapeDtypeStruct((t, d), _BF16),
                   jax.ShapeDtypeStruct((t, qw), _BF16)),
        grid=(t // tm,),
        in_specs=[rows(d), rows(d), whole(w_out), pl.BlockSpec((1, d), lambda i: (0, 0)), whole(wq)],
        out_specs=(rows(d), rows(d), rows(qw)),
        compiler_params=_params(("arbitrary",)),
        name="outproj",
    )(x2, merged, w_out, gain.reshape(1, d), wq)


_CAND_WIDE = 4
_CAND_TAIL = tuple((p1, p2) for p1 in range(_CAND_WIDE, PEER_TOPK) for p2 in range(PEER_TOPK)
                   if (p1 + 1) * (p2 + 1) <= PEER_TOPK)
_SUBLANES = 8


def _top_rows(vals, k, exact, want_rank):
    n = vals.shape[0]
    row = lax.broadcasted_iota(jnp.int32, vals.shape, 0).astype(_F32) if exact else None
    rank = jnp.full(vals.shape, float(k), _F32) if want_rank else None
    work = vals
    tops = []
    for r in range(k):
        best = jnp.max(work, axis=0, keepdims=True)
        hit = work == best
        if exact:
            first = jnp.min(jnp.where(hit, row, float(n)), axis=0, keepdims=True)
            hit = row == first
        if want_rank:
            rank = jnp.where(hit, float(r), rank)
        work = jnp.where(hit, -jnp.inf, work)
        tops.append(best)
    return jnp.concatenate(tops, axis=0), rank, work


def _extracted(left):
    return jnp.sum(jnp.where(left == -jnp.inf, 1.0, 0.0), axis=0, keepdims=True)


def _rank_keys(s1, s2, exact):
    k = PEER_TOPK
    top1, rank1, left1 = _top_rows(s1, k, exact, want_rank=exact)
    top2, rank2, left2 = _top_rows(s2, k, exact, want_rank=True)
    e2 = jnp.exp(s2 - top2[0:1])
    bad = None if exact else jnp.abs(_extracted(left1) - k) + jnp.abs(_extracted(left2) - k)
    return top1, top2, rank1, rank2, e2, bad


def _rank_pairs(s1, top1, top2, rank1, exact):
    k = PEER_TOPK
    n_pad = -len(_CAND_TAIL) % _SUBLANES
    pad = [jnp.full_like(top1[0:1], -jnp.inf)] * n_pad
    cand = jnp.concatenate([top1[p:p + 1] + top2 for p in range(_CAND_WIDE)]
                           + [top1[p1:p1 + 1] + top2[p2:p2 + 1] for p1, p2 in _CAND_TAIL] + pad, axis=0)
    top_c, _, left_c = _top_rows(cand, k, exact, want_rank=False)
    chosen = jnp.where(left_c == -jnp.inf, 1.0, 0.0)
    per_rank = [jnp.sum(chosen[p * k:(p + 1) * k], axis=0, keepdims=True) for p in range(_CAND_WIDE)]
    for p1 in range(_CAND_WIDE, k):
        rows = [_CAND_WIDE * k + i for i, (a, _) in enumerate(_CAND_TAIL) if a == p1]
        per_rank.append(jnp.sum(chosen[rows[0]:rows[-1] + 1], axis=0, keepdims=True))
    count = jnp.zeros(s1.shape, _F32)
    for p in range(k):
        count = jnp.where((rank1 == p) if exact else (s1 == top1[p:p + 1]), per_rank[p], count)
    z = jnp.sum(jnp.exp(top_c - top_c[0:1]), axis=0, keepdims=True)
    coef = jnp.exp(s1 - top1[0:1]) / z
    bad = None if exact else jnp.abs(_extracted(left_c) - (k + n_pad))
    return count, coef, bad


def _select_body(q_ref, k1_ref, k2_ref, cnt_ref, coef_ref, rank_ref, e2_ref, s1_scr, s2_scr):
    q = q_ref[...]
    s1_scr[...] = lax.dot_general(k1_ref[...], q[:, :PEER_HALF], _NT, preferred_element_type=_F32)
    s2_scr[...] = lax.dot_general(k2_ref[...], q[:, PEER_HALF:], _NT, preferred_element_type=_F32)
    n_tiles = s1_scr.shape[1] // LANES

    def lanes_of(i):
        return pl.ds(pl.multiple_of(i * LANES, LANES), LANES)

    def keys_stage(i, exact):
        lanes = lanes_of(i)
        top1, top2, rank1, rank2, e2, bad = _rank_keys(s1_scr[:, lanes], s2_scr[:, lanes], exact)
        rank_ref[:, lanes] = rank2.astype(rank_ref.dtype)
        e2_ref[:, lanes] = e2.astype(e2_ref.dtype)
        return top1, top2, rank1, bad

    def pairs_stage(i, top1, top2, rank1, exact):
        lanes = lanes_of(i)
        count, coef, bad = _rank_pairs(s1_scr[:, lanes], top1, top2, rank1, exact)
        cnt_ref[:, lanes] = count
        coef_ref[:, lanes] = coef
        return bad

    top1, top2, _, bad = keys_stage(0, exact=False)

    def skewed(i, carry):
        top1, top2, bad = carry
        nxt1, nxt2, _, bad_keys = keys_stage(i + 1, exact=False)
        bad_pairs = pairs_stage(i, top1, top2, None, exact=False)
        return nxt1, nxt2, jnp.maximum(bad, jnp.maximum(bad_keys, bad_pairs))

    top1, top2, bad = lax.fori_loop(0, n_tiles - 1, skewed, (top1, top2, bad))
    bad = jnp.maximum(bad, pairs_stage(n_tiles - 1, top1, top2, None, exact=False))

    @pl.when(jnp.max(bad) > 0.0)
    def _():
        def exact_tile(i, carry):
            top1, top2, rank1, _ = keys_stage(i, exact=True)
            pairs_stage(i, top1, top2, rank1, exact=True)
            return carry

        lax.fori_loop(0, n_tiles, exact_tile, 0)


def _select(q, keys1, keys2):
    t = q.shape[0]
    tm = min(1024, t)
    qd = 2 * PEER_HALF
    out = jax.ShapeDtypeStruct((PEER_HEADS, N_KEYS, t), _F32)
    out16 = jax.ShapeDtypeStruct((PEER_HEADS, N_KEYS, t), _BF16)
    spec = pl.BlockSpec((None, N_KEYS, tm), lambda i, h: (h, 0, i))
    return pl.pallas_call(
        _select_body,
        out_shape=(out, out, out16, out16),
        grid=(t // tm, PEER_HEADS),
        in_specs=[
            pl.BlockSpec((tm, qd), lambda i, h: (i, h)),
            pl.BlockSpec((None, N_KEYS, PEER_HALF), lambda i, h: (h, 0, 0)),
            pl.BlockSpec((None, N_KEYS, PEER_HALF), lambda i, h: (h, 0, 0)),
        ],
        out_specs=(spec, spec, spec, spec),
        scratch_shapes=[pltpu.VMEM((N_KEYS, tm), _F32), pltpu.VMEM((N_KEYS, tm), _F32)],
        compiler_params=_params(("arbitrary", "arbitrary")),
        name="select",
    )(q, keys1, keys2)


def _peer_body(h2_ref, down_ref, upt_ref, cnt_ref, coef_ref, rank_ref, e2_ref, x1_ref, o_ref,
               acc_scr, wg_scr, *, key_rows):
    j = pl.program_id(1)

    @pl.when(j == 0)
    def _():
        acc_scr[...] = jnp.zeros_like(acc_scr)

    act = lax.dot_general(down_ref[...], h2_ref[...], _NT, preferred_element_type=_F32)
    gact = _gelu(act).astype(_BF16)
    block = (N_KEYS, gact.shape[1])
    zero = jnp.zeros(block, _BF16)
    for a in range(key_rows):
        rows = slice(a * N_KEYS, (a + 1) * N_KEYS)
        w = None
        for h in range(PEER_HEADS):
            cnt = jnp.broadcast_to(cnt_ref[h, a:a + 1, :].astype(_BF16), block)
            coef = jnp.broadcast_to(coef_ref[h, a:a + 1, :].astype(_BF16), block)
            term = jnp.where(rank_ref[h] < cnt, e2_ref[h], zero) * coef
            w = term if w is None else w + term
        wg_scr[rows, :] = w * gact[rows, :]
    acc_scr[...] += jnp.dot(upt_ref[...], wg_scr[...], preferred_element_type=_F32)

    @pl.when(j == pl.num_programs(1) - 1)
    def _():
        o_ref[...] = x1_ref[...] + acc_scr[...].T


def _peer(h2, down, up_t, count, coef, rank2, e2, x1):
    t, d = h2.shape
    n_e = down.shape[0]
    tm, te = min(512, t), 1024
    key_rows = te // N_KEYS
    once = pl.Buffered(1)
    return pl.pallas_call(
        functools.partial(_peer_body, key_rows=key_rows),
        out_shape=jax.ShapeDtypeStruct((t, d), _F32),
        grid=(t // tm, n_e // te),
        in_specs=[
            pl.BlockSpec((tm, d), lambda i, j: (i, 0), pipeline_mode=once),
            pl.BlockSpec((te, d), lambda i, j: (j, 0)),
            pl.BlockSpec((d, te), lambda i, j: (0, j)),
            pl.BlockSpec((PEER_HEADS, key_rows, tm), lambda i, j: (0, j, i)),
            pl.BlockSpec((PEER_HEADS, key_rows, tm), lambda i, j: (0, j, i)),
            pl.BlockSpec((PEER_HEADS, N_KEYS, tm), lambda i, j: (0, 0, i), pipeline_mode=once),
            pl.BlockSpec((PEER_HEADS, N_KEYS, tm), lambda i, j: (0, 0, i), pipeline_mode=once),
            pl.BlockSpec((tm, d), lambda i, j: (i, 0), pipeline_mode=once),
        ],
        out_specs=pl.BlockSpec((tm, d), lambda i, j: (i, 0)),
        scratch_shapes=[pltpu.VMEM((d, tm), _F32), pltpu.VMEM((te, tm), _BF16)],
        compiler_params=_params(("arbitrary", "arbitrary")),
        name="peer",
    )(h2, down, up_t, count, coef, rank2, e2, x1)


def _layer(x, mix_g, w_in, q_g, k_g, sink, w_attn_o, a_re, a_im, log_step, b_re, b_im, c_re, c_im, d_skip,
           glu_a, glu_b, w_out, ffn_g, wq, keys1, keys2, down, up):
    b, s, d = x.shape
    t = b * s
    aw, kw = N_Q_HEADS * HEAD_DIM, N_KV_HEADS * HEAD_DIM
    sw = d_skip.shape[0]
    u_col = aw + 2 * kw
    ga_col = u_col + sw
    gb_col = ga_col + d
    x2 = x.reshape(t, d)

    proj = _inproj(x2, mix_g, w_in.astype(_BF16))
    attn = _attention(proj.reshape(b, s, -1), q_g, k_g, sink).reshape(t, aw)

    m, s_in, s_out, lam = _ssm_matrices(a_re, a_im, log_step, b_re, b_im, c_re, c_im, d_skip)
    yact = _ssm(proj, u_col, m, s_in, s_out, lam, b, s)

    merged = _merge(attn, yact, proj, ga_col, gb_col,
                    w_attn_o.astype(_BF16), glu_a.astype(_BF16), glu_b.astype(_BF16))
    x1, h2, q = _outproj(x2, merged, w_out.astype(_BF16), ffn_g, wq.astype(_BF16))

    count, coef, rank2, e2 = _select(q, keys1.astype(_BF16), keys2.astype(_BF16))
    out = _peer(h2, down.astype(_BF16), up.T.astype(_BF16), count, coef, rank2, e2, x1)
    return out.reshape(b, s, d)


def kernel(x, mix_norm_g, w_in, q_norm_g, k_norm_g, attn_sink, w_attn_o, ssm_a_re, ssm_a_im, ssm_log_step,
           ssm_b_re, ssm_b_im, ssm_c_re, ssm_c_im, ssm_d, glu_w_a, glu_w_b, w_out, ffn_norm_g, peer_w_query,
           peer_sub_keys_1, peer_sub_keys_2, peer_down, peer_up):
    for l in range(mix_norm_g.shape[0]):
        x = _layer(x, mix_norm_g[l], w_in[l], q_norm_g[l], k_norm_g[l], attn_sink[l], w_attn_o[l],
                   ssm_a_re[l], ssm_a_im[l], ssm_log_step[l], ssm_b_re[l], ssm_b_im[l], ssm_c_re[l],
                   ssm_c_im[l], ssm_d[l], glu_w_a[l], glu_w_b[l], w_out[l], ffn_norm_g[l], peer_w_query[l],
                   peer_sub_keys_1[l], peer_sub_keys_2[l], peer_down[l], peer_up[l])
    return x
```

```python
import functools
import math

import jax
import jax.numpy as jnp
from jax import lax
from jax.experimental import pallas as pl
from jax.experimental.pallas import tpu as pltpu

N_Q_HEADS = 16
N_KV_HEADS = 4
Q_GROUP = N_Q_HEADS // N_KV_HEADS
HEAD_DIM = 128
WINDOW = 128
BLOCK = 128
SSM_GROUP = 16
SSM_STATE = 64
SSM_CHUNK = 16
PEER_HEADS = 8
N_KEYS = 128
PEER_HALF = 128
PEER_TOPK = 16
RMS_EPS = 1e-6
LANES = 128
VMEM_LIMIT = 56 * 1024 * 1024

_F32 = jnp.float32
_BF16 = jnp.bfloat16
_NT = (((1,), (1,)), ((), ()))


def _params(semantics):
    return pltpu.CompilerParams(dimension_semantics=semantics, vmem_limit_bytes=VMEM_LIMIT)


def _gelu(v):
    return 0.5 * v * (1.0 + lax.erf(v * (1.0 / math.sqrt(2.0))))


def _rms_scale(v):
    return lax.rsqrt(jnp.mean(v * v, axis=-1, keepdims=True) + RMS_EPS)


def _inproj_body(x_ref, g_ref, w_ref, o_ref, h_scr):
    @pl.when(pl.program_id(1) == 0)
    def _():
        x = x_ref[...]
        h_scr[...] = (x * _rms_scale(x) * g_ref[...]).astype(h_scr.dtype)

    o_ref[...] = jnp.dot(h_scr[...], w_ref[...], preferred_element_type=_F32).astype(o_ref.dtype)


def _inproj(x2, gain, w_bf16):
    t, d = x2.shape
    n = w_bf16.shape[1]
    tm, tn = min(1024, t), 1024
    return pl.pallas_call(
        _inproj_body,
        out_shape=jax.ShapeDtypeStruct((t, n), _BF16),
        grid=(t // tm, n // tn),
        in_specs=[
            pl.BlockSpec((tm, d), lambda i, j: (i, 0)),
            pl.BlockSpec((1, d), lambda i, j: (0, 0)),
            pl.BlockSpec((d, tn), lambda i, j: (0, j)),
        ],
        out_specs=pl.BlockSpec((tm, tn), lambda i, j: (i, j)),
        scratch_shapes=[pltpu.VMEM((tm, d), _BF16)],
        compiler_params=_params(("arbitrary", "arbitrary")),
        name="inproj",
    )(x2, gain.reshape(1, d), w_bf16)


def _attn_body(sink_ref, q_ref, kp_ref, kc_ref, kn_ref, vp_ref, vc_ref, vn_ref, qg_ref, kg_ref, o_ref,
               *, seq_len, slopes):
    i = pl.program_id(1)
    row = lax.broadcasted_iota(jnp.int32, (BLOCK, 3 * BLOCK), 0)
    col = lax.broadcasted_iota(jnp.int32, (BLOCK, 3 * BLOCK), 1)
    dist = jnp.abs(row + BLOCK - col)
    kpos = (i - 1) * BLOCK + col
    valid = (dist <= WINDOW) & (kpos >= 0) & (kpos < seq_len)
    neg_dist = jnp.where(valid, -dist.astype(_F32), -jnp.inf)

    k3 = jnp.concatenate([kp_ref[...], kc_ref[...], kn_ref[...]], axis=0).astype(_F32)
    v3 = jnp.concatenate([vp_ref[...], vc_ref[...], vn_ref[...]], axis=0)
    qf = q_ref[...].astype(_F32)
    q_gain = qg_ref[...] * (HEAD_DIM ** -0.5)
    k_gain = kg_ref[...]

    for kv in range(N_KV_HEADS):
        kh = k3[:, kv * HEAD_DIM:(kv + 1) * HEAD_DIM]
        kh = (kh * _rms_scale(kh) * k_gain).astype(_BF16)
        q_rows = []
        for g in range(Q_GROUP):
            h = kv * Q_GROUP + g
            qh = qf[:, h * HEAD_DIM:(h + 1) * HEAD_DIM]
            q_rows.append((qh * _rms_scale(qh) * q_gain).astype(_BF16))
        scores = lax.dot_general(jnp.concatenate(q_rows, axis=0), kh, _NT,
                                 preferred_element_type=_F32)
        probs, denoms = [], []
        for g in range(Q_GROUP):
            h = kv * Q_GROUP + g
            s = scores[g * BLOCK:(g + 1) * BLOCK] + slopes[h] * neg_dist
            sink = sink_ref[h]
            m = jnp.maximum(jnp.max(s, axis=-1, keepdims=True), sink)
            p = jnp.exp(s - m)
            denoms.append(jnp.sum(p, axis=-1, keepdims=True) + jnp.exp(sink - m))
            probs.append(p.astype(_BF16))
        pv = jnp.dot(jnp.concatenate(probs, axis=0), v3[:, kv * HEAD_DIM:(kv + 1) * HEAD_DIM],
                     preferred_element_type=_F32)
        for g in range(Q_GROUP):
            h = kv * Q_GROUP + g
            o_ref[:, h * HEAD_DIM:(h + 1) * HEAD_DIM] = (
                pv[g * BLOCK:(g + 1) * BLOCK] / denoms[g]).astype(o_ref.dtype)


def _attention(proj3, q_gain, k_gain, sink):
    b, s, _ = proj3.shape
    nb = s // BLOCK
    aw, kw = N_Q_HEADS * HEAD_DIM, N_KV_HEADS * HEAD_DIM
    k_col, v_col = aw // kw, aw // kw + 1
    slopes = tuple(2.0 ** (-8.0 * (h + 1.0) / N_Q_HEADS) for h in range(N_Q_HEADS))

    def kv_spec(col, shift):
        return pl.BlockSpec((None, BLOCK, kw),
                            lambda bi, i: (bi, jnp.clip(i + shift, 0, nb - 1), col))

    return pl.pallas_call(
        functools.partial(_attn_body, seq_len=s, slopes=slopes),
        out_shape=jax.ShapeDtypeStruct((b, s, aw), _BF16),
        grid=(b, nb),
        in_specs=[
            pl.BlockSpec(memory_space=pltpu.SMEM),
            pl.BlockSpec((None, BLOCK, aw), lambda bi, i: (bi, i, 0)),
            kv_spec(k_col, -1), kv_spec(k_col, 0), kv_spec(k_col, 1),
            kv_spec(v_col, -1), kv_spec(v_col, 0), kv_spec(v_col, 1),
            pl.BlockSpec((1, HEAD_DIM), lambda bi, i: (0, 0)),
            pl.BlockSpec((1, HEAD_DIM), lambda bi, i: (0, 0)),
        ],
        out_specs=pl.BlockSpec((None, BLOCK, aw), lambda bi, i: (bi, i, 0)),
        compiler_params=_params(("arbitrary", "arbitrary")),
        name="attn",
    )(sink.astype(_F32), proj3, proj3, proj3, proj3, proj3, proj3, proj3,
      q_gain.reshape(1, HEAD_DIM).astype(_F32), k_gain.reshape(1, HEAD_DIM).astype(_F32))


def _ssm_matrices(a_re, a_im, log_step, b_re, b_im, c_re, c_im, d_skip):
    hp = lax.Precision.HIGHEST
    n_l, n_c, n_s = SSM_CHUNK, SSM_GROUP, SSM_STATE
    n_g = a_re.shape[1]
    a_re, a_im, b_re, b_im, c_re, c_im = (v.astype(_F32) for v in (a_re, a_im, b_re, b_im, c_re, c_im))
    step = jnp.exp(log_step.astype(_F32))[..., None]
    steps = jnp.arange(n_l + 1, dtype=_F32)[:, None, None, None]
    mag = jnp.exp((a_re * step)[None] * steps)
    ang = (a_im * step)[None] * steps
    p_re, p_im = mag * jnp.cos(ang), mag * jnp.sin(ang)
    num_re, num_im = p_re[1] - 1.0, p_im[1]
    den = a_re * a_re + a_im * a_im
    f_re = ((num_re * a_re + num_im * a_im) / den)[..., None]
    f_im = ((num_im * a_re - num_re * a_im) / den)[..., None]
    bb_re, bb_im = f_re * b_re - f_im * b_im, f_re * b_im + f_im * b_re
    pe_re, pe_im = p_re[:, :, :, None, :], p_im[:, :, :, None, :]
    cp_re = c_re[None] * pe_re - c_im[None] * pe_im
    cp_im = c_re[None] * pe_im + c_im[None] * pe_re

    kern = (jnp.einsum('tdgcn,dgne->dtgce', cp_re[:n_l], bb_re, precision=hp)
            - jnp.einsum('tdgcn,dgne->dtgce', cp_im[:n_l], bb_im, precision=hp))
    gp = LANES // n_c
    n_sg = n_g // gp
    skip = jnp.eye(n_c, dtype=_F32)[None] * d_skip.astype(_F32).reshape(n_g, n_c, 1)
    lagged = jnp.concatenate([kern[1][:0:-1], (kern[0][0] + kern[1][0] + skip)[None], kern[0][1:]], axis=0)
    lag = jnp.arange(n_l)[None, :] - jnp.arange(n_l)[:, None] + (n_l - 1)
    select_lag = (lag[:, :, None] == jnp.arange(2 * n_l - 1)[None, None, :]).astype(_F32)
    m = jnp.einsum('jJt,tshCc->sjcJhC', select_lag, lagged.reshape(2 * n_l - 1, n_sg, gp, n_c, n_c))
    m = m.reshape(n_sg, n_l * n_c, n_l * gp * n_c)

    def state_in(direction, backwards):
        q_re, q_im = p_re[:n_l, direction], p_im[:n_l, direction]
        if backwards:
            q_re, q_im = q_re[::-1], q_im[::-1]
        q_re, q_im = q_re[:, :, None, :], q_im[:, :, None, :]
        t_re, t_im = bb_re[direction].transpose(0, 2, 1)[None], bb_im[direction].transpose(0, 2, 1)[None]
        return q_re * t_re - q_im * t_im, q_re * t_im + q_im * t_re

    w_in_g = jnp.stack(state_in(0, True) + state_in(1, False), axis=3)
    w_in_g = w_in_g.transpose(1, 0, 2, 3, 4)
    out_f = (cp_re[1:n_l + 1, 0], -cp_im[1:n_l + 1, 0])
    out_b = (cp_re[n_l:0:-1, 1], -cp_im[n_l:0:-1, 1])
    w_out_g = jnp.stack([z.transpose(1, 3, 0, 2) for z in out_f + out_b], axis=1)
    lam_g = jnp.stack([p_re[n_l, 0], p_im[n_l, 0], p_re[n_l, 1], p_im[n_l, 1]], axis=1)

    def per_tile(z):
        _, a, b, c, d = z.shape
        return z.reshape(n_sg, gp, a, b, c, d).transpose(0, 2, 3, 4, 1, 5).reshape(n_sg, a * b, c * gp * d)

    w_in, w_out = per_tile(w_in_g), per_tile(w_out_g)
    lam = lam_g.reshape(n_sg, gp, 4, n_s).transpose(0, 2, 1, 3).reshape(n_sg, 4, gp * n_s)
    return m.astype(_BF16), w_in.astype(_BF16), w_out.astype(_BF16), lam


def _expand_block_diagonal(compact_ref, full_scr, block_rows, col_group):
    gp = LANES // SSM_GROUP
    width = compact_ref.shape[1]
    group_of_col = (lax.broadcasted_iota(jnp.int32, (block_rows, width), 1) // col_group) % gp
    zero = jnp.zeros((block_rows, width), compact_ref.dtype)
    for a in range(compact_ref.shape[0] // block_rows):
        block = compact_ref[a * block_rows:(a + 1) * block_rows, :]
        for g in range(gp):
            full_scr[(a * gp + g) * block_rows:(a * gp + g + 1) * block_rows, :] = jnp.where(group_of_col == g, block, zero)


def _ssm_body(x_ref, m_ref, win_ref, wout_ref, lam_ref, o_ref, lhs_scr, st_scr, io_scr, m_scr, win_scr, wout_scr,
              *, n_chunks, n_batch):
    n_l = SSM_CHUNK
    rows = n_batch * n_chunks
    pairs = n_l // 2
    hi = jnp.uint32(0xFFFF0000)

    @pl.when(pl.program_id(1) == 0)
    def _():
        _expand_block_diagonal(m_ref, m_scr, SSM_GROUP, SSM_GROUP)
        _expand_block_diagonal(win_ref, win_scr, SSM_GROUP, SSM_STATE)
        _expand_block_diagonal(wout_ref, wout_scr, SSM_STATE, SSM_GROUP)

    io_scr[...] = pltpu.bitcast(x_ref[...], jnp.uint32)
    for jj in range(pairs):
        w = io_scr[pl.ds(jj, rows, stride=pairs), :]
        even = pltpu.bitcast(w << 16, _F32)
        odd = pltpu.bitcast(w & hi, _F32)
        lhs_scr[:, (2 * jj) * LANES:(2 * jj + 1) * LANES] = even.astype(_BF16)
        lhs_scr[:, (2 * jj + 1) * LANES:(2 * jj + 2) * LANES] = odd.astype(_BF16)

    inc = jnp.dot(lhs_scr[...], win_scr[...], preferred_element_type=_F32)
    n_tiles = st_scr.shape[0]
    q = n_tiles // 4
    for t in range(n_tiles):
        st_scr[t] = inc[:, t * LANES:(t + 1) * LANES]
    lam = lam_ref[...]
    coef = [[jnp.broadcast_to(lam[p:p + 1, i * LANES:(i + 1) * LANES], (n_batch, LANES)) for i in range(q)]
            for p in range(4)]

    def step(k, carry):
        rf = pl.ds(k, n_batch, stride=n_chunks)
        rb = pl.ds(n_chunks - 1 - k, n_batch, stride=n_chunks)
        new = list(carry)
        for i in range(q):
            for base, r in ((0, rf), (2 * q, rb)):
                t_re, t_im = base + i, base + q + i
                a_re, a_im = coef[base // q][i], coef[base // q + 1][i]
                s_re, s_im = carry[t_re], carry[t_im]
                d_re, d_im = st_scr[t_re, r, :], st_scr[t_im, r, :]
                st_scr[t_re, r, :] = s_re
                st_scr[t_im, r, :] = s_im
                new[t_re] = a_re * s_re - a_im * s_im + d_re
                new[t_im] = a_re * s_im + a_im * s_re + d_im
        return tuple(new)

    zero = jnp.zeros((n_batch, LANES), _F32)
    lax.fori_loop(0, n_chunks, step, (zero,) * n_tiles, unroll=True)
    states = jnp.concatenate([st_scr[t] for t in range(n_tiles)], axis=1).astype(_BF16)
    y = (jnp.dot(lhs_scr[...], m_scr[...], preferred_element_type=_F32)
         + jnp.dot(states, wout_scr[...], preferred_element_type=_F32))
    y = _gelu(y).astype(_BF16).astype(_F32)

    for jj in range(pairs):
        even = pltpu.bitcast(y[:, (2 * jj) * LANES:(2 * jj + 1) * LANES], jnp.uint32)
        odd = pltpu.bitcast(y[:, (2 * jj + 1) * LANES:(2 * jj + 2) * LANES], jnp.uint32)
        io_scr[pl.ds(jj, rows, stride=pairs), :] = (odd & hi) | (even >> 16)
    o_ref[...] = pltpu.bitcast(io_scr[...], _BF16)


def _ssm(proj, u_col, m, w_in, w_out, lam, n_batch_total, seq_len):
    t = proj.shape[0]
    n_sg, _, width = m.shape
    bt = 4
    n_chunks = seq_len // SSM_CHUNK
    rows = bt * n_chunks
    n_state = w_in.shape[2]
    n_tiles = n_state // LANES
    col0 = u_col // LANES
    compact = lambda arr: pl.BlockSpec((None,) + arr.shape[1:], lambda s, i: (s, 0, 0),
                                       pipeline_mode=pl.Buffered(1))
    return pl.pallas_call(
        functools.partial(_ssm_body, n_chunks=n_chunks, n_batch=bt),
        out_shape=jax.ShapeDtypeStruct((t, n_sg * LANES), _BF16),
        grid=(n_sg, n_batch_total // bt),
        in_specs=[
            pl.BlockSpec((bt * seq_len, LANES), lambda s, i: (i, col0 + s)),
            compact(m), compact(w_in), compact(w_out), compact(lam),
        ],
        out_specs=pl.BlockSpec((bt * seq_len, LANES), lambda s, i: (i, s)),
        scratch_shapes=[pltpu.VMEM((rows, width), _BF16), pltpu.VMEM((n_tiles, rows, LANES), _F32),
                        pltpu.VMEM((bt * seq_len // 2, LANES), jnp.uint32),
                        pltpu.VMEM((width, width), _BF16), pltpu.VMEM((width, n_state), _BF16),
                        pltpu.VMEM((n_state, width), _BF16)],
        compiler_params=_params(("arbitrary", "arbitrary")),
        name="ssm",
    )(proj, m, w_in, w_out, lam)


def _merge_body(attn_ref, yact_ref, ga_ref, gb_ref, wo_ref, wa_ref, wb_ref, o_ref):
    y_a = jnp.dot(attn_ref[...], wo_ref[...], preferred_element_type=_F32)
    yact = yact_ref[...]
    glu = (jnp.dot(yact, wa_ref[...], preferred_element_type=_F32)
           * jax.nn.sigmoid(jnp.dot(yact, wb_ref[...], preferred_element_type=_F32)))
    o_ref[...] = (jax.nn.sigmoid(ga_ref[...].astype(_F32)) * y_a
                  + jax.nn.sigmoid(gb_ref[...].astype(_F32)) * glu).astype(o_ref.dtype)


def _merge(attn2, yact2, proj, gate_a_col, gate_b_col, wo, wa, wb):
    t, aw = attn2.shape
    sw = yact2.shape[1]
    d = wo.shape[1]
    tm, tn = min(512, t), d
    ga0, gb0 = gate_a_col // tn, gate_b_col // tn
    once = pl.Buffered(1)
    return pl.pallas_call(
        _merge_body,
        out_shape=jax.ShapeDtypeStruct((t, d), _BF16),
        grid=(t // tm, d // tn),
        in_specs=[
            pl.BlockSpec((tm, aw), lambda i, j: (i, 0)),
            pl.BlockSpec((tm, sw), lambda i, j: (i, 0)),
            pl.BlockSpec((tm, tn), lambda i, j: (i, ga0 + j)),
            pl.BlockSpec((tm, tn), lambda i, j: (i, gb0 + j)),
            pl.BlockSpec((aw, tn), lambda i, j: (0, j), pipeline_mode=once),
            pl.BlockSpec((sw, tn), lambda i, j: (0, j), pipeline_mode=once),
            pl.BlockSpec((sw, tn), lambda i, j: (0, j), pipeline_mode=once),
        ],
        out_specs=pl.BlockSpec((tm, tn), lambda i, j: (i, j)),
        compiler_params=_params(("arbitrary", "arbitrary")),
        name="merge",
    )(attn2, yact2, proj, proj, wo, wa, wb)


def _outproj_body(x_ref, m_ref, w_ref, g_ref, wq_ref, x1_ref, h2_ref, q_ref):
    x1 = x_ref[...] + jnp.dot(m_ref[...], w_ref[...], preferred_element_type=_F32)
    x1_ref[...] = x1
    h2 = (x1 * _rms_scale(x1) * g_ref[...]).astype(h2_ref.dtype)
    h2_ref[...] = h2
    q_ref[...] = jnp.dot(h2, wq_ref[...], preferred_element_type=_F32).astype(q_ref.dtype)


def _outproj(x2, merged, w_out, gain, wq):
    t, d = x2.shape
    qw = wq.shape[1]
    tm = min(512, t)
    rows = lambda width: pl.BlockSpec((tm, width), lambda i: (i, 0))
    whole = lambda arr: pl.BlockSpec(arr.shape, lambda i: (0, 0))
    return pl.pallas_call(
        _outproj_body,
        out_shape=(jax.ShapeDtypeStruct((t, d), _F32), jax.ShapeDtypeStruct((t, d), _BF16),
                   jax.ShapeDtypeStruct((t, qw), _BF16)),
        grid=(t // tm,),
        in_specs=[rows(d), rows(d), whole(w_out), pl.BlockSpec((1, d), lambda i: (0, 0)), whole(wq)],
        out_specs=(rows(d), rows(d), rows(qw)),
        compiler_params=_params(("arbitrary",)),
        name="outproj",
    )(x2, merged, w_out, gain.reshape(1, d), wq)


_CAND_WIDE = 4
_CAND_TAIL = tuple((p1, p2) for p1 in range(_CAND_WIDE, PEER_TOPK) for p2 in range(PEER_TOPK)
                   if (p1 + 1) * (p2 + 1) <= PEER_TOPK)
_SUBLANES = 8


def _top_rows(vals, k, exact, want_rank):
    n = vals.shape[0]
    row = lax.broadcasted_iota(jnp.int32, vals.shape, 0).astype(_F32) if exact else None
    rank = jnp.full(vals.shape, float(k), _F32) if want_rank else None
    work = vals
    tops = []
    for r in range(k):
        best = jnp.max(work, axis=0, keepdims=True)
        hit = work == best
        if exact:
            first = jnp.min(jnp.where(hit, row, float(n)), axis=0, keepdims=True)
            hit = row == first
        if want_rank:
            rank = jnp.where(hit, float(r), rank)
        work = jnp.where(hit, -jnp.inf, work)
        tops.append(best)
    return jnp.concatenate(tops, axis=0), rank, work


def _extracted(left):
    return jnp.sum(jnp.where(left == -jnp.inf, 1.0, 0.0), axis=0, keepdims=True)


def _rank_keys(s1, s2, exact):
    k = PEER_TOPK
    top1, rank1, left1 = _top_rows(s1, k, exact, want_rank=exact)
    top2, rank2, left2 = _top_rows(s2, k, exact, want_rank=True)
    e2 = jnp.exp(s2 - top2[0:1])
    bad = None if exact else jnp.abs(_extracted(left1) - k) + jnp.abs(_extracted(left2) - k)
    return top1, top2, rank1, rank2, e2, bad


def _rank_pairs(s1, top1, top2, rank1, exact):
    k = PEER_TOPK
    n_pad = -len(_CAND_TAIL) % _SUBLANES
    pad = [jnp.full_like(top1[0:1], -jnp.inf)] * n_pad
    cand = jnp.concatenate([top1[p:p + 1] + top2 for p in range(_CAND_WIDE)]
                           + [top1[p1:p1 + 1] + top2[p2:p2 + 1] for p1, p2 in _CAND_TAIL] + pad, axis=0)
    top_c, _, left_c = _top_rows(cand, k, exact, want_rank=False)
    chosen = jnp.where(left_c == -jnp.inf, 1.0, 0.0)
    per_rank = [jnp.sum(chosen[p * k:(p + 1) * k], axis=0, keepdims=True) for p in range(_CAND_WIDE)]
    for p1 in range(_CAND_WIDE, k):
        rows = [_CAND_WIDE * k + i for i, (a, _) in enumerate(_CAND_TAIL) if a == p1]
        per_rank.append(jnp.sum(chosen[rows[0]:rows[-1] + 1], axis=0, keepdims=True))
    count = jnp.zeros(s1.shape, _F32)
    for p in range(k):
        count = jnp.where((rank1 == p) if exact else (s1 == top1[p:p + 1]), per_rank[p], count)
    z = jnp.sum(jnp.exp(top_c - top_c[0:1]), axis=0, keepdims=True)
    coef = jnp.exp(s1 - top1[0:1]) / z
    bad = None if exact else jnp.abs(_extracted(left_c) - (k + n_pad))
    return count, coef, bad


def _select_body(q_ref, k1_ref, k2_ref, cnt_ref, coef_ref, rank_ref, e2_ref, s1_scr, s2_scr):
    q = q_ref[...]
    s1_scr[...] = lax.dot_general(k1_ref[...], q[:, :PEER_HALF], _NT, preferred_element_type=_F32)
    s2_scr[...] = lax.dot_general(k2_ref[...], q[:, PEER_HALF:], _NT, preferred_element_type=_F32)
    n_tiles = s1_scr.shape[1] // LANES

    def lanes_of(i):
        return pl.ds(pl.multiple_of(i * LANES, LANES), LANES)

    def keys_stage(i, exact):
        lanes = lanes_of(i)
        top1, top2, rank1, rank2, e2, bad = _rank_keys(s1_scr[:, lanes], s2_scr[:, lanes], exact)
        rank_ref[:, lanes] = rank2.astype(rank_ref.dtype)
        e2_ref[:, lanes] = e2.astype(e2_ref.dtype)
        return top1, top2, rank1, bad

    def pairs_stage(i, top1, top2, rank1, exact):
        lanes = lanes_of(i)
        count, coef, bad = _rank_pairs(s1_scr[:, lanes], top1, top2, rank1, exact)
        cnt_ref[:, lanes] = count
        coef_ref[:, lanes] = coef
        return bad

    top1, top2, _, bad = keys_stage(0, exact=False)

    def skewed(i, carry):
        top1, top2, bad = carry
        nxt1, nxt2, _, bad_keys = keys_stage(i + 1, exact=False)
        bad_pairs = pairs_stage(i, top1, top2, None, exact=False)
        return nxt1, nxt2, jnp.maximum(bad, jnp.maximum(bad_keys, bad_pairs))

    top1, top2, bad = lax.fori_loop(0, n_tiles - 1, skewed, (top1, top2, bad))
    bad = jnp.maximum(bad, pairs_stage(n_tiles - 1, top1, top2, None, exact=False))

    @pl.when(jnp.max(bad) > 0.0)
    def _():
        def exact_tile(i, carry):
            top1, top2, rank1, _ = keys_stage(i, exact=True)
            pairs_stage(i, top1, top2, rank1, exact=True)
            return carry

        lax.fori_loop(0, n_tiles, exact_tile, 0)


def _select(q, keys1, keys2):
    t = q.shape[0]
    tm = min(512, t)
    qd = 2 * PEER_HALF
    out = jax.ShapeDtypeStruct((PEER_HEADS, N_KEYS, t), _F32)
    out16 = jax.ShapeDtypeStruct((PEER_HEADS, N_KEYS, t), _BF16)
    spec = pl.BlockSpec((None, N_KEYS, tm), lambda i, h: (h, 0, i))
    return pl.pallas_call(
        _select_body,
        out_shape=(out, out, out16, out16),
        grid=(t // tm, PEER_HEADS),
        in_specs=[
            pl.BlockSpec((tm, qd), lambda i, h: (i, h)),
            pl.BlockSpec((None, N_KEYS, PEER_HALF), lambda i, h: (h, 0, 0)),
            pl.BlockSpec((None, N_KEYS, PEER_HALF), lambda i, h: (h, 0, 0)),
        ],
        out_specs=(spec, spec, spec, spec),
        scratch_shapes=[pltpu.VMEM((N_KEYS, tm), _F32), pltpu.VMEM((N_KEYS, tm), _F32)],
        compiler_params=_params(("arbitrary", "arbitrary")),
        name="select",
    )(q, keys1, keys2)


def _peer_body(h2_ref, down_ref, upt_ref, cnt_ref, coef_ref, rank_ref, e2_ref, x1_ref, o_ref,
               acc_scr, wg_scr, *, key_rows):
    j = pl.program_id(1)

    @pl.when(j == 0)
    def _():
        acc_scr[...] = jnp.zeros_like(acc_scr)

    act = lax.dot_general(down_ref[...], h2_ref[...], _NT, preferred_element_type=_F32)
    gact = _gelu(act).astype(_BF16)
    block = (N_KEYS, gact.shape[1])
    zero = jnp.zeros(block, _BF16)
    for a in range(key_rows):
        rows = slice(a * N_KEYS, (a + 1) * N_KEYS)
        w = None
        for h in range(PEER_HEADS):
            cnt = jnp.broadcast_to(cnt_ref[h, a:a + 1, :].astype(_BF16), block)
            coef = jnp.broadcast_to(coef_ref[h, a:a + 1, :].astype(_BF16), block)
            term = jnp.where(rank_ref[h] < cnt, e2_ref[h], zero) * coef
            w = term if w is None else w + term
        wg_scr[rows, :] = w * gact[rows, :]
    acc_scr[...] += jnp.dot(upt_ref[...], wg_scr[...], preferred_element_type=_F32)

    @pl.when(j == pl.num_programs(1) - 1)
    def _():
        o_ref[...] = x1_ref[...] + acc_scr[...].T


def _peer(h2, down, up_t, count, coef, rank2, e2, x1):
    t, d = h2.shape
    n_e = down.shape[0]
    tm, te = min(512, t), 1024
    key_rows = te // N_KEYS
    once = pl.Buffered(1)
    return pl.pallas_call(
        functools.partial(_peer_body, key_rows=key_rows),
        out_shape=jax.ShapeDtypeStruct((t, d), _F32),
        grid=(t // tm, n_e // te),
        in_specs=[
            pl.BlockSpec((tm, d), lambda i, j: (i, 0), pipeline_mode=once),
            pl.BlockSpec((te, d), lambda i, j: (j, 0)),
            pl.BlockSpec((d, te), lambda i, j: (0, j)),
            pl.BlockSpec((PEER_HEADS, key_rows, tm), lambda i, j: (0, j, i)),
            pl.BlockSpec((PEER_HEADS, key_rows, tm), lambda i, j: (0, j, i)),
            pl.BlockSpec((PEER_HEADS, N_KEYS, tm), lambda i, j: (0, 0, i), pipeline_mode=once),
            pl.BlockSpec((PEER_HEADS, N_KEYS, tm), lambda i, j: (0, 0, i), pipeline_mode=once),
            pl.BlockSpec((tm, d), lambda i, j: (i, 0), pipeline_mode=once),
        ],
        out_specs=pl.BlockSpec((tm, d), lambda i, j: (i, 0)),
        scratch_shapes=[pltpu.VMEM((d, tm), _F32), pltpu.VMEM((te, tm), _BF16)],
        compiler_params=_params(("arbitrary", "arbitrary")),
        name="peer",
    )(h2, down, up_t, count, coef, rank2, e2, x1)


def _layer(x, mix_g, w_in, q_g, k_g, sink, w_attn_o, a_re, a_im, log_step, b_re, b_im, c_re, c_im, d_skip,
           glu_a, glu_b, w_out, ffn_g, wq, keys1, keys2, down, up):
    b, s, d = x.shape
    t = b * s
    aw, kw = N_Q_HEADS * HEAD_DIM, N_KV_HEADS * HEAD_DIM
    sw = d_skip.shape[0]
    u_col = aw + 2 * kw
    ga_col = u_col + sw
    gb_col = ga_col + d
    x2 = x.reshape(t, d)

    proj = _inproj(x2, mix_g, w_in.astype(_BF16))
    attn = _attention(proj.reshape(b, s, -1), q_g, k_g, sink).reshape(t, aw)

    m, s_in, s_out, lam = _ssm_matrices(a_re, a_im, log_step, b_re, b_im, c_re, c_im, d_skip)
    yact = _ssm(proj, u_col, m, s_in, s_out, lam, b, s)

    merged = _merge(attn, yact, proj, ga_col, gb_col,
                    w_attn_o.astype(_BF16), glu_a.astype(_BF16), glu_b.astype(_BF16))
    x1, h2, q = _outproj(x2, merged, w_out.astype(_BF16), ffn_g, wq.astype(_BF16))

    count, coef, rank2, e2 = _select(q, keys1.astype(_BF16), keys2.astype(_BF16))
    out = _peer(h2, down.astype(_BF16), up.T.astype(_BF16), count, coef, rank2, e2, x1)
    return out.reshape(b, s, d)


def kernel(x, mix_norm_g, w_in, q_norm_g, k_norm_g, attn_sink, w_attn_o, ssm_a_re, ssm_a_im, ssm_log_step,
           ssm_b_re, ssm_b_im, ssm_c_re, ssm_c_im, ssm_d, glu_w_a, glu_w_b, w_out, ffn_norm_g, peer_w_query,
           peer_sub_keys_1, peer_sub_keys_2, peer_down, peer_up):
    for l in range(mix_norm_g.shape[0]):
        x = _layer(x, mix_norm_g[l], w_in[l], q_norm_g[l], k_norm_g[l], attn_sink[l], w_attn_o[l],
                   ssm_a_re[l], ssm_a_im[l], ssm_log_step[l], ssm_b_re[l], ssm_b_im[l], ssm_c_re[l],
                   ssm_c_im[l], ssm_d[l], glu_w_a[l], glu_w_b[l], w_out[l], ffn_norm_g[l], peer_w_query[l],
                   peer_sub_keys_1[l], peer_sub_keys_2[l], peer_down[l], peer_up[l])
    return x
```

```python
import functools
import math

import jax
import jax.numpy as jnp
from jax import lax
from jax.experimental import pallas as pl
from jax.experimental.pallas import tpu as pltpu

N_Q_HEADS = 16
N_KV_HEADS = 4
Q_GROUP = N_Q_HEADS // N_KV_HEADS
HEAD_DIM = 128
WINDOW = 128
BLOCK = 128
SSM_GROUP = 16
SSM_STATE = 64
SSM_CHUNK = 16
PEER_HEADS = 8
N_KEYS = 128
PEER_HALF = 128
PEER_TOPK = 16
RMS_EPS = 1e-6
LANES = 128
VMEM_LIMIT = 56 * 1024 * 1024

_F32 = jnp.float32
_BF16 = jnp.bfloat16
_NT = (((1,), (1,)), ((), ()))


def _params(semantics):
    return pltpu.CompilerParams(dimension_semantics=semantics, vmem_limit_bytes=VMEM_LIMIT)


def _gelu(v):
    return 0.5 * v * (1.0 + lax.erf(v * (1.0 / math.sqrt(2.0))))


def _rms_scale(v):
    return lax.rsqrt(jnp.mean(v * v, axis=-1, keepdims=True) + RMS_EPS)


def _inproj_body(x_ref, g_ref, w_ref, o_ref, h_scr):
    @pl.when(pl.program_id(1) == 0)
    def _():
        x = x_ref[...]
        h_scr[...] = (x * _rms_scale(x) * g_ref[...]).astype(h_scr.dtype)

    o_ref[...] = jnp.dot(h_scr[...], w_ref[...], preferred_element_type=_F32).astype(o_ref.dtype)


def _inproj(x2, gain, w_bf16):
    t, d = x2.shape
    n = w_bf16.shape[1]
    tm, tn = min(1024, t), 1024
    return pl.pallas_call(
        _inproj_body,
        out_shape=jax.ShapeDtypeStruct((t, n), _BF16),
        grid=(t // tm, n // tn),
        in_specs=[
            pl.BlockSpec((tm, d), lambda i, j: (i, 0)),
            pl.BlockSpec((1, d), lambda i, j: (0, 0)),
            pl.BlockSpec((d, tn), lambda i, j: (0, j)),
        ],
        out_specs=pl.BlockSpec((tm, tn), lambda i, j: (i, j)),
        scratch_shapes=[pltpu.VMEM((tm, d), _BF16)],
        compiler_params=_params(("arbitrary", "arbitrary")),
        name="inproj",
    )(x2, gain.reshape(1, d), w_bf16)


def _attn_body(sink_ref, q_ref, kp_ref, kc_ref, kn_ref, vp_ref, vc_ref, vn_ref, qg_ref, kg_ref, o_ref,
               *, seq_len, slopes):
    i = pl.program_id(1)
    row = lax.broadcasted_iota(jnp.int32, (BLOCK, 3 * BLOCK), 0)
    col = lax.broadcasted_iota(jnp.int32, (BLOCK, 3 * BLOCK), 1)
    dist = jnp.abs(row + BLOCK - col)
    kpos = (i - 1) * BLOCK + col
    valid = (dist <= WINDOW) & (kpos >= 0) & (kpos < seq_len)
    neg_dist = jnp.where(valid, -dist.astype(_F32), -jnp.inf)

    k3 = jnp.concatenate([kp_ref[...], kc_ref[...], kn_ref[...]], axis=0).astype(_F32)
    v3 = jnp.concatenate([vp_ref[...], vc_ref[...], vn_ref[...]], axis=0)
    qf = q_ref[...].astype(_F32)
    q_gain = qg_ref[...] * (HEAD_DIM ** -0.5)
    k_gain = kg_ref[...]

    for kv in range(N_KV_HEADS):
        kh = k3[:, kv * HEAD_DIM:(kv + 1) * HEAD_DIM]
        kh = (kh * _rms_scale(kh) * k_gain).astype(_BF16)
        q_rows = []
        for g in range(Q_GROUP):
            h = kv * Q_GROUP + g
            qh = qf[:, h * HEAD_DIM:(h + 1) * HEAD_DIM]
            q_rows.append((qh * _rms_scale(qh) * q_gain).astype(_BF16))
        scores = lax.dot_general(jnp.concatenate(q_rows, axis=0), kh, _NT,
                                 preferred_element_type=_F32)
        probs, denoms = [], []
        for g in range(Q_GROUP):
            h = kv * Q_GROUP + g
            s = scores[g * BLOCK:(g + 1) * BLOCK] + slopes[h] * neg_dist
            sink = sink_ref[h]
            m = jnp.maximum(jnp.max(s, axis=-1, keepdims=True), sink)
            p = jnp.exp(s - m)
            denoms.append(jnp.sum(p, axis=-1, keepdims=True) + jnp.exp(sink - m))
            probs.append(p.astype(_BF16))
        pv = jnp.dot(jnp.concatenate(probs, axis=0), v3[:, kv * HEAD_DIM:(kv + 1) * HEAD_DIM],
                     preferred_element_type=_F32)
        for g in range(Q_GROUP):
            h = kv * Q_GROUP + g
            o_ref[:, h * HEAD_DIM:(h + 1) * HEAD_DIM] = (
                pv[g * BLOCK:(g + 1) * BLOCK] / denoms[g]).astype(o_ref.dtype)


def _attention(proj3, q_gain, k_gain, sink):
    b, s, _ = proj3.shape
    nb = s // BLOCK
    aw, kw = N_Q_HEADS * HEAD_DIM, N_KV_HEADS * HEAD_DIM
    k_col, v_col = aw // kw, aw // kw + 1
    slopes = tuple(2.0 ** (-8.0 * (h + 1.0) / N_Q_HEADS) for h in range(N_Q_HEADS))

    def kv_spec(col, shift):
        return pl.BlockSpec((None, BLOCK, kw),
                            lambda bi, i: (bi, jnp.clip(i + shift, 0, nb - 1), col))

    return pl.pallas_call(
        functools.partial(_attn_body, seq_len=s, slopes=slopes),
        out_shape=jax.ShapeDtypeStruct((b, s, aw), _BF16),
        grid=(b, nb),
        in_specs=[
            pl.BlockSpec(memory_space=pltpu.SMEM),
            pl.BlockSpec((None, BLOCK, aw), lambda bi, i: (bi, i, 0)),
            kv_spec(k_col, -1), kv_spec(k_col, 0), kv_spec(k_col, 1),
            kv_spec(v_col, -1), kv_spec(v_col, 0), kv_spec(v_col, 1),
            pl.BlockSpec((1, HEAD_DIM), lambda bi, i: (0, 0)),
            pl.BlockSpec((1, HEAD_DIM), lambda bi, i: (0, 0)),
        ],
        out_specs=pl.BlockSpec((None, BLOCK, aw), lambda bi, i: (bi, i, 0)),
        compiler_params=_params(("arbitrary", "arbitrary")),
        name="attn",
    )(sink.astype(_F32), proj3, proj3, proj3, proj3, proj3, proj3, proj3,
      q_gain.reshape(1, HEAD_DIM).astype(_F32), k_gain.reshape(1, HEAD_DIM).astype(_F32))


def _ssm_matrices(a_re, a_im, log_step, b_re, b_im, c_re, c_im, d_skip):
    n_l, n_c, n_s = SSM_CHUNK, SSM_GROUP, SSM_STATE
    n_g = a_re.shape[1]
    a_re, a_im, b_re, b_im, c_re, c_im = (v.astype(_F32) for v in (a_re, a_im, b_re, b_im, c_re, c_im))
    step = jnp.exp(log_step.astype(_F32))[..., None]
    steps = jnp.arange(n_l + 1, dtype=_F32)[:, None, None, None]
    mag = jnp.exp((a_re * step)[None] * steps)
    ang = (a_im * step)[None] * steps
    p_re, p_im = mag * jnp.cos(ang), mag * jnp.sin(ang)
    num_re, num_im = p_re[1] - 1.0, p_im[1]
    den = a_re * a_re + a_im * a_im
    f_re = ((num_re * a_re + num_im * a_im) / den)[..., None]
    f_im = ((num_im * a_re - num_re * a_im) / den)[..., None]
    bb_re, bb_im = f_re * b_re - f_im * b_im, f_re * b_im + f_im * b_re
    pe_re, pe_im = p_re[:, :, :, None, :], p_im[:, :, :, None, :]
    cp_re = c_re[None] * pe_re - c_im[None] * pe_im
    cp_im = c_re[None] * pe_im + c_im[None] * pe_re

    kern = jnp.sum(cp_re[:n_l, :, :, :, :, None] * bb_re[None, :, :, None, :, :]
                   - cp_im[:n_l, :, :, :, :, None] * bb_im[None, :, :, None, :, :], axis=4).transpose(1, 0, 2, 3, 4)
    gp = LANES // n_c
    n_sg = n_g // gp
    skip = jnp.eye(n_c, dtype=_F32)[None] * d_skip.astype(_F32).reshape(n_g, n_c, 1)
    lagged = jnp.concatenate([kern[1][:0:-1], (kern[0][0] + kern[1][0] + skip)[None], kern[0][1:]], axis=0)
    lag = jnp.arange(n_l)[None, :] - jnp.arange(n_l)[:, None] + (n_l - 1)
    select_lag = (lag[:, :, None] == jnp.arange(2 * n_l - 1)[None, None, :]).astype(_F32)
    m = jnp.einsum('jJt,tshCc->sjcJhC', select_lag, lagged.reshape(2 * n_l - 1, n_sg, gp, n_c, n_c))
    m = m.reshape(n_sg, n_l * n_c, n_l * gp * n_c)

    def state_in(direction, backwards):
        q_re, q_im = p_re[:n_l, direction], p_im[:n_l, direction]
        if backwards:
            q_re, q_im = q_re[::-1], q_im[::-1]
        q_re, q_im = q_re[:, :, None, :], q_im[:, :, None, :]
        t_re, t_im = bb_re[direction].transpose(0, 2, 1)[None], bb_im[direction].transpose(0, 2, 1)[None]
        return q_re * t_re - q_im * t_im, q_re * t_im + q_im * t_re

    w_in_g = jnp.stack(state_in(0, True) + state_in(1, False), axis=3)
    w_in_g = w_in_g.transpose(1, 0, 2, 3, 4)
    out_f = (cp_re[1:n_l + 1, 0], -cp_im[1:n_l + 1, 0])
    out_b = (cp_re[n_l:0:-1, 1], -cp_im[n_l:0:-1, 1])
    w_out_g = jnp.stack([z.transpose(1, 3, 0, 2) for z in out_f + out_b], axis=1)
    lam_g = jnp.stack([p_re[n_l, 0], p_im[n_l, 0], p_re[n_l, 1], p_im[n_l, 1]], axis=1)

    def per_tile(z):
        _, a, b, c, d = z.shape
        return z.reshape(n_sg, gp, a, b, c, d).transpose(0, 2, 3, 4, 1, 5).reshape(n_sg, a * b, c * gp * d)

    w_in, w_out = per_tile(w_in_g), per_tile(w_out_g)
    lam = lam_g.reshape(n_sg, gp, 4, n_s).transpose(0, 2, 1, 3).reshape(n_sg, 4, gp * n_s)
    return m.astype(_BF16), w_in.astype(_BF16), w_out.astype(_BF16), lam


def _expand_block_diagonal(compact_ref, full_scr, block_rows, col_group):
    gp = LANES // SSM_GROUP
    width = compact_ref.shape[1]
    group_of_col = (lax.broadcasted_iota(jnp.int32, (block_rows, width), 1) // col_group) % gp
    zero = jnp.zeros((block_rows, width), compact_ref.dtype)
    for a in range(compact_ref.shape[0] // block_rows):
        block = compact_ref[a * block_rows:(a + 1) * block_rows, :]
        for g in range(gp):
            full_scr[(a * gp + g) * block_rows:(a * gp + g + 1) * block_rows, :] = jnp.where(group_of_col == g, block, zero)


def _ssm_body(x_ref, m_ref, win_ref, wout_ref, lam_ref, o_ref, lhs_scr, st_scr, io_scr, m_scr, win_scr, wout_scr,
              *, n_chunks, n_batch):
    n_l = SSM_CHUNK
    rows = n_batch * n_chunks

    @pl.when(pl.program_id(1) == 0)
    def _():
        _expand_block_diagonal(m_ref, m_scr, SSM_GROUP, SSM_GROUP)
        _expand_block_diagonal(win_ref, win_scr, SSM_GROUP, SSM_STATE)
        _expand_block_diagonal(wout_ref, wout_scr, SSM_STATE, SSM_GROUP)

    io_scr[...] = x_ref[...].astype(_F32)
    for j in range(n_l):
        lhs_scr[:, j * LANES:(j + 1) * LANES] = io_scr[pl.ds(j, rows, stride=n_l), :].astype(_BF16)

    inc = jnp.dot(lhs_scr[...], win_scr[...], preferred_element_type=_F32)
    n_tiles = st_scr.shape[0]
    q = n_tiles // 4
    for t in range(n_tiles):
        st_scr[t] = inc[:, t * LANES:(t + 1) * LANES]
    lam = lam_ref[...]
    coef = [[jnp.broadcast_to(lam[p:p + 1, i * LANES:(i + 1) * LANES], (n_batch, LANES)) for i in range(q)]
            for p in range(4)]

    def step(k, carry):
        rf = pl.ds(k, n_batch, stride=n_chunks)
        rb = pl.ds(n_chunks - 1 - k, n_batch, stride=n_chunks)
        new = list(carry)
        for i in range(q):
            for base, r in ((0, rf), (2 * q, rb)):
                t_re, t_im = base + i, base + q + i
                a_re, a_im = coef[base // q][i], coef[base // q + 1][i]
                s_re, s_im = carry[t_re], carry[t_im]
                d_re, d_im = st_scr[t_re, r, :], st_scr[t_im, r, :]
                st_scr[t_re, r, :] = s_re
                st_scr[t_im, r, :] = s_im
                new[t_re] = a_re * s_re - a_im * s_im + d_re
                new[t_im] = a_re * s_im + a_im * s_re + d_im
        return tuple(new)

    zero = jnp.zeros((n_batch, LANES), _F32)
    lax.fori_loop(0, n_chunks, step, (zero,) * n_tiles, unroll=True)
    states = jnp.concatenate([st_scr[t] for t in range(n_tiles)], axis=1).astype(_BF16)
    y = (jnp.dot(lhs_scr[...], m_scr[...], preferred_element_type=_F32)
         + jnp.dot(states, wout_scr[...], preferred_element_type=_F32))
    y = _gelu(y)
    for j in range(n_l):
        io_scr[pl.ds(j, rows, stride=n_l), :] = y[:, j * LANES:(j + 1) * LANES]
    o_ref[...] = io_scr[...].astype(o_ref.dtype)


def _ssm(proj, u_col, m, w_in, w_out, lam, n_batch_total, seq_len):
    t = proj.shape[0]
    n_sg, _, width = m.shape
    bt = 4
    n_chunks = seq_len // SSM_CHUNK
    rows = bt * n_chunks
    n_state = w_in.shape[2]
    n_tiles = n_state // LANES
    col0 = u_col // LANES
    compact = lambda arr: pl.BlockSpec((None,) + arr.shape[1:], lambda s, i: (s, 0, 0),
                                       pipeline_mode=pl.Buffered(1))
    return pl.pallas_call(
        functools.partial(_ssm_body, n_chunks=n_chunks, n_batch=bt),
        out_shape=jax.ShapeDtypeStruct((t, n_sg * LANES), _BF16),
        grid=(n_sg, n_batch_total // bt),
        in_specs=[
            pl.BlockSpec((bt * seq_len, LANES), lambda s, i: (i, col0 + s)),
            compact(m), compact(w_in), compact(w_out), compact(lam),
        ],
        out_specs=pl.BlockSpec((bt * seq_len, LANES), lambda s, i: (i, s)),
        scratch_shapes=[pltpu.VMEM((rows, width), _BF16), pltpu.VMEM((n_tiles, rows, LANES), _F32),
                        pltpu.VMEM((bt * seq_len, LANES), _F32),
                        pltpu.VMEM((width, width), _BF16), pltpu.VMEM((width, n_state), _BF16),
                        pltpu.VMEM((n_state, width), _BF16)],
        compiler_params=_params(("arbitrary", "arbitrary")),
        name="ssm",
    )(proj, m, w_in, w_out, lam)


def _merge_body(attn_ref, yact_ref, ga_ref, gb_ref, wo_ref, wa_ref, wb_ref, o_ref):
    y_a = jnp.dot(attn_ref[...], wo_ref[...], preferred_element_type=_F32)
    yact = yact_ref[...]
    glu = (jnp.dot(yact, wa_ref[...], preferred_element_type=_F32)
           * jax.nn.sigmoid(jnp.dot(yact, wb_ref[...], preferred_element_type=_F32)))
    o_ref[...] = (jax.nn.sigmoid(ga_ref[...].astype(_F32)) * y_a
                  + jax.nn.sigmoid(gb_ref[...].astype(_F32)) * glu).astype(o_ref.dtype)


def _merge(attn2, yact2, proj, gate_a_col, gate_b_col, wo, wa, wb):
    t, aw = attn2.shape
    sw = yact2.shape[1]
    d = wo.shape[1]
    tm, tn = min(512, t), d
    ga0, gb0 = gate_a_col // tn, gate_b_col // tn
    once = pl.Buffered(1)
    return pl.pallas_call(
        _merge_body,
        out_shape=jax.ShapeDtypeStruct((t, d), _BF16),
        grid=(t // tm, d // tn),
        in_specs=[
            pl.BlockSpec((tm, aw), lambda i, j: (i, 0)),
            pl.BlockSpec((tm, sw), lambda i, j: (i, 0)),
            pl.BlockSpec((tm, tn), lambda i, j: (i, ga0 + j)),
            pl.BlockSpec((tm, tn), lambda i, j: (i, gb0 + j)),
            pl.BlockSpec((aw, tn), lambda i, j: (0, j), pipeline_mode=once),
            pl.BlockSpec((sw, tn), lambda i, j: (0, j), pipeline_mode=once),
            pl.BlockSpec((sw, tn), lambda i, j: (0, j), pipeline_mode=once),
        ],
        out_specs=pl.BlockSpec((tm, tn), lambda i, j: (i, j)),
        compiler_params=_params(("arbitrary", "arbitrary")),
        name="merge",
    )(attn2, yact2, proj, proj, wo, wa, wb)


def _outproj_body(x_ref, m_ref, w_ref, g_ref, wq_ref, x1_ref, h2_ref, q_ref):
    x1 = x_ref[...] + jnp.dot(m_ref[...], w_ref[...], preferred_element_type=_F32)
    x1_ref[...] = x1
    h2 = (x1 * _rms_scale(x1) * g_ref[...]).astype(h2_ref.dtype)
    h2_ref[...] = h2
    q_ref[...] = jnp.dot(h2, wq_ref[...], preferred_element_type=_F32).astype(q_ref.dtype)


def _outproj(x2, merged, w_out, gain, wq):
    t, d = x2.shape
    qw = wq.shape[1]
    tm = min(512, t)
    rows = lambda width: pl.BlockSpec((tm, width), lambda i: (i, 0))
    whole = lambda arr: pl.BlockSpec(arr.shape, lambda i: (0, 0))
    return pl.pallas_call(
        _outproj_body,
        out_shape=(jax.ShapeDtypeStruct((t, d), _F32), jax.ShapeDtypeStruct((t, d), _BF16),
                   jax.ShapeDtypeStruct((t, qw), _BF16)),
        grid=(t // tm,),
        in_specs=[rows(d), rows(d), whole(w_out), pl.BlockSpec((1, d), lambda i: (0, 0)), whole(wq)],
        out_specs=(rows(d), rows(d), rows(qw)),
        compiler_params=_params(("arbitrary",)),
        name="outproj",
    )(x2, merged, w_out, gain.reshape(1, d), wq)


_CAND_WIDE = 4
_CAND_TAIL = tuple((p1, p2) for p1 in range(_CAND_WIDE, PEER_TOPK) for p2 in range(PEER_TOPK)
                   if (p1 + 1) * (p2 + 1) <= PEER_TOPK)
_SUBLANES = 8


def _top_rows(vals, k, exact, want_rank):
    n = vals.shape[0]
    row = lax.broadcasted_iota(jnp.int32, vals.shape, 0).astype(_F32) if exact else None
    rank = jnp.full(vals.shape, float(k), _F32) if want_rank else None
    work = vals
    tops = []
    for r in range(k):
        best = jnp.max(work, axis=0, keepdims=True)
        hit = work == best
        if exact:
            first = jnp.min(jnp.where(hit, row, float(n)), axis=0, keepdims=True)
            hit = row == first
        if want_rank:
            rank = jnp.where(hit, float(r), rank)
        work = jnp.where(hit, -jnp.inf, work)
        tops.append(best)
    return jnp.concatenate(tops, axis=0), rank, work


def _extracted(left):
    return jnp.sum(jnp.where(left == -jnp.inf, 1.0, 0.0), axis=0, keepdims=True)


def _rank_keys(s1, s2, exact):
    k = PEER_TOPK
    top1, rank1, left1 = _top_rows(s1, k, exact, want_rank=exact)
    top2, rank2, left2 = _top_rows(s2, k, exact, want_rank=True)
    e2 = jnp.exp(s2 - top2[0:1])
    bad = None if exact else jnp.abs(_extracted(left1) - k) + jnp.abs(_extracted(left2) - k)
    return top1, top2, rank1, rank2, e2, bad


def _rank_pairs(s1, top1, top2, rank1, exact):
    k = PEER_TOPK
    n_pad = -len(_CAND_TAIL) % _SUBLANES
    pad = [jnp.full_like(top1[0:1], -jnp.inf)] * n_pad
    cand = jnp.concatenate([top1[p:p + 1] + top2 for p in range(_CAND_WIDE)]
                           + [top1[p1:p1 + 1] + top2[p2:p2 + 1] for p1, p2 in _CAND_TAIL] + pad, axis=0)
    top_c, _, left_c = _top_rows(cand, k, exact, want_rank=False)
    chosen = jnp.where(left_c == -jnp.inf, 1.0, 0.0)
    per_rank = [jnp.sum(chosen[p * k:(p + 1) * k], axis=0, keepdims=True) for p in range(_CAND_WIDE)]
    for p1 in range(_CAND_WIDE, k):
        rows = [_CAND_WIDE * k + i for i, (a, _) in enumerate(_CAND_TAIL) if a == p1]
        per_rank.append(jnp.sum(chosen[rows[0]:rows[-1] + 1], axis=0, keepdims=True))
    count = jnp.zeros(s1.shape, _F32)
    for p in range(k):
        count = jnp.where((rank1 == p) if exact else (s1 == top1[p:p + 1]), per_rank[p], count)
    z = jnp.sum(jnp.exp(top_c - top_c[0:1]), axis=0, keepdims=True)
    coef = jnp.exp(s1 - top1[0:1]) / z
    bad = None if exact else jnp.abs(_extracted(left_c) - (k + n_pad))
    return count, coef, bad


def _select_body(q_ref, k1_ref, k2_ref, cnt_ref, coef_ref, rank_ref, e2_ref, s1_scr, s2_scr):
    q = q_ref[...]
    s1_scr[...] = lax.dot_general(k1_ref[...], q[:, :PEER_HALF], _NT, preferred_element_type=_F32)
    s2_scr[...] = lax.dot_general(k2_ref[...], q[:, PEER_HALF:], _NT, preferred_element_type=_F32)
    n_tiles = s1_scr.shape[1] // LANES

    def lanes_of(i):
        return pl.ds(pl.multiple_of(i * LANES, LANES), LANES)

    def keys_stage(i, exact):
        lanes = lanes_of(i)
        top1, top2, rank1, rank2, e2, bad = _rank_keys(s1_scr[:, lanes], s2_scr[:, lanes], exact)
        rank_ref[:, lanes] = rank2.astype(rank_ref.dtype)
        e2_ref[:, lanes] = e2.astype(e2_ref.dtype)
        return top1, top2, rank1, bad

    def pairs_stage(i, top1, top2, rank1, exact):
        lanes = lanes_of(i)
        count, coef, bad = _rank_pairs(s1_scr[:, lanes], top1, top2, rank1, exact)
        cnt_ref[:, lanes] = count
        coef_ref[:, lanes] = coef
        return bad

    top1, top2, _, bad = keys_stage(0, exact=False)

    def skewed(i, carry):
        top1, top2, bad = carry
        nxt1, nxt2, _, bad_keys = keys_stage(i + 1, exact=False)
        bad_pairs = pairs_stage(i, top1, top2, None, exact=False)
        return nxt1, nxt2, jnp.maximum(bad, jnp.maximum(bad_keys, bad_pairs))

    top1, top2, bad = lax.fori_loop(0, n_tiles - 1, skewed, (top1, top2, bad))
    bad = jnp.maximum(bad, pairs_stage(n_tiles - 1, top1, top2, None, exact=False))

    @pl.when(jnp.max(bad) > 0.0)
    def _():
        def exact_tile(i, carry):
            top1, top2, rank1, _ = keys_stage(i, exact=True)
            pairs_stage(i, top1, top2, rank1, exact=True)
            return carry

        lax.fori_loop(0, n_tiles, exact_tile, 0)


def _select(q, keys1, keys2):
    t = q.shape[0]
    tm = min(512, t)
    qd = 2 * PEER_HALF
    out = jax.ShapeDtypeStruct((PEER_HEADS, N_KEYS, t), _F32)
    out16 = jax.ShapeDtypeStruct((PEER_HEADS, N_KEYS, t), _BF16)
    spec = pl.BlockSpec((None, N_KEYS, tm), lambda i, h: (h, 0, i))
    return pl.pallas_call(
        _select_body,
        out_shape=(out, out, out16, out16),
        grid=(t // tm, PEER_HEADS),
        in_specs=[
            pl.BlockSpec((tm, qd), lambda i, h: (i, h)),
            pl.BlockSpec((None, N_KEYS, PEER_HALF), lambda i, h: (h, 0, 0)),
            pl.BlockSpec((None, N_KEYS, PEER_HALF), lambda i, h: (h, 0, 0)),
        ],
        out_specs=(spec, spec, spec, spec),
        scratch_shapes=[pltpu.VMEM((N_KEYS, tm), _F32), pltpu.VMEM((N_KEYS, tm), _F32)],
        compiler_params=_params(("arbitrary", "arbitrary")),
        name="select",
    )(q, keys1, keys2)


def _peer_body(h2_ref, down_ref, upt_ref, cnt_ref, coef_ref, rank_ref, e2_ref, x1_ref, o_ref,
               acc_scr, wg_scr, *, key_rows):
    j = pl.program_id(1)

    @pl.when(j == 0)
    def _():
        acc_scr[...] = jnp.zeros_like(acc_scr)

    act = lax.dot_general(down_ref[...], h2_ref[...], _NT, preferred_element_type=_F32)
    gact = _gelu(act).astype(_BF16)
    block = (N_KEYS, gact.shape[1])
    zero = jnp.zeros(block, _BF16)
    for a in range(key_rows):
        rows = slice(a * N_KEYS, (a + 1) * N_KEYS)
        w = None
        for h in range(PEER_HEADS):
            cnt = jnp.broadcast_to(cnt_ref[h, a:a + 1, :].astype(_BF16), block)
            coef = jnp.broadcast_to(coef_ref[h, a:a + 1, :].astype(_BF16), block)
            term = jnp.where(rank_ref[h] < cnt, e2_ref[h], zero) * coef
            w = term if w is None else w + term
        wg_scr[rows, :] = w * gact[rows, :]
    acc_scr[...] += jnp.dot(upt_ref[...], wg_scr[...], preferred_element_type=_F32)

    @pl.when(j == pl.num_programs(1) - 1)
    def _():
        o_ref[...] = x1_ref[...] + acc_scr[...].T


def _peer(h2, down, up_t, count, coef, rank2, e2, x1):
    t, d = h2.shape
    n_e = down.shape[0]
    tm, te = min(512, t), 1024
    key_rows = te // N_KEYS
    once = pl.Buffered(1)
    return pl.pallas_call(
        functools.partial(_peer_body, key_rows=key_rows),
        out_shape=jax.ShapeDtypeStruct((t, d), _F32),
        grid=(t // tm, n_e // te),
        in_specs=[
            pl.BlockSpec((tm, d), lambda i, j: (i, 0), pipeline_mode=once),
            pl.BlockSpec((te, d), lambda i, j: (j, 0)),
            pl.BlockSpec((d, te), lambda i, j: (0, j)),
            pl.BlockSpec((PEER_HEADS, key_rows, tm), lambda i, j: (0, j, i)),
            pl.BlockSpec((PEER_HEADS, key_rows, tm), lambda i, j: (0, j, i)),
            pl.BlockSpec((PEER_HEADS, N_KEYS, tm), lambda i, j: (0, 0, i), pipeline_mode=once),
            pl.BlockSpec((PEER_HEADS, N_KEYS, tm), lambda i, j: (0, 0, i), pipeline_mode=once),
            pl.BlockSpec((tm, d), lambda i, j: (i, 0), pipeline_mode=once),
        ],
        out_specs=pl.BlockSpec((tm, d), lambda i, j: (i, 0)),
        scratch_shapes=[pltpu.VMEM((d, tm), _F32), pltpu.VMEM((te, tm), _BF16)],
        compiler_params=_params(("arbitrary", "arbitrary")),
        name="peer",
    )(h2, down, up_t, count, coef, rank2, e2, x1)


def _layer(x, mix_g, w_in, q_g, k_g, sink, w_attn_o, a_re, a_im, log_step, b_re, b_im, c_re, c_im, d_skip,
           glu_a, glu_b, w_out, ffn_g, wq, keys1, keys2, down, up):
    b, s, d = x.shape
    t = b * s
    aw, kw = N_Q_HEADS * HEAD_DIM, N_KV_HEADS * HEAD_DIM
    sw = d_skip.shape[0]
    u_col = aw + 2 * kw
    ga_col = u_col + sw
    gb_col = ga_col + d
    x2 = x.reshape(t, d)

    proj = _inproj(x2, mix_g, w_in.astype(_BF16))
    attn = _attention(proj.reshape(b, s, -1), q_g, k_g, sink).reshape(t, aw)

    m, s_in, s_out, lam = _ssm_matrices(a_re, a_im, log_step, b_re, b_im, c_re, c_im, d_skip)
    yact = _ssm(proj, u_col, m, s_in, s_out, lam, b, s)

    merged = _merge(attn, yact, proj, ga_col, gb_col,
                    w_attn_o.astype(_BF16), glu_a.astype(_BF16), glu_b.astype(_BF16))
    x1, h2, q = _outproj(x2, merged, w_out.astype(_BF16), ffn_g, wq.astype(_BF16))

    count, coef, rank2, e2 = _select(q, keys1.astype(_BF16), keys2.astype(_BF16))
    out = _peer(h2, down.astype(_BF16), up.T.astype(_BF16), count, coef, rank2, e2, x1)
    return out.reshape(b, s, d)


def kernel(x, mix_norm_g, w_in, q_norm_g, k_norm_g, attn_sink, w_attn_o, ssm_a_re, ssm_a_im, ssm_log_step,
           ssm_b_re, ssm_b_im, ssm_c_re, ssm_c_im, ssm_d, glu_w_a, glu_w_b, w_out, ffn_norm_g, peer_w_query,
           peer_sub_keys_1, peer_sub_keys_2, peer_down, peer_up):
    for l in range(mix_norm_g.shape[0]):
        x = _layer(x, mix_norm_g[l], w_in[l], q_norm_g[l], k_norm_g[l], attn_sink[l], w_attn_o[l],
                   ssm_a_re[l], ssm_a_im[l], ssm_log_step[l], ssm_b_re[l], ssm_b_im[l], ssm_c_re[l],
                   ssm_c_im[l], ssm_d[l], glu_w_a[l], glu_w_b[l], w_out[l], ffn_norm_g[l], peer_w_query[l],
                   peer_sub_keys_1[l], peer_sub_keys_2[l], peer_down[l], peer_up[l])
    return x
```

```python
import functools
import math

import jax
import jax.numpy as jnp
from jax import lax
from jax.experimental import pallas as pl
from jax.experimental.pallas import tpu as pltpu

N_Q_HEADS = 16
N_KV_HEADS = 4
Q_GROUP = N_Q_HEADS // N_KV_HEADS
HEAD_DIM = 128
WINDOW = 128
BLOCK = 128
SSM_GROUP = 16
SSM_STATE = 64
SSM_CHUNK = 16
PEER_HEADS = 8
N_KEYS = 128
PEER_HALF = 128
PEER_TOPK = 16
RMS_EPS = 1e-6
LANES = 128
VMEM_LIMIT = 56 * 1024 * 1024

_F32 = jnp.float32
_BF16 = jnp.bfloat16
_NT = (((1,), (1,)), ((), ()))


def _params(semantics):
    return pltpu.CompilerParams(dimension_semantics=semantics, vmem_limit_bytes=VMEM_LIMIT)


def _gelu(v):
    return 0.5 * v * (1.0 + lax.erf(v * (1.0 / math.sqrt(2.0))))


def _rms_scale(v):
    return lax.rsqrt(jnp.mean(v * v, axis=-1, keepdims=True) + RMS_EPS)


def _inproj_body(x_ref, g_ref, w_ref, o_ref, h_scr):
    @pl.when(pl.program_id(1) == 0)
    def _():
        x = x_ref[...]
        h_scr[...] = (x * _rms_scale(x) * g_ref[...]).astype(h_scr.dtype)

    o_ref[...] = jnp.dot(h_scr[...], w_ref[...], preferred_element_type=_F32).astype(o_ref.dtype)


def _inproj(x2, gain, w_bf16):
    t, d = x2.shape
    n = w_bf16.shape[1]
    tm, tn = min(1024, t), 1024
    return pl.pallas_call(
        _inproj_body,
        out_shape=jax.ShapeDtypeStruct((t, n), _BF16),
        grid=(t // tm, n // tn),
        in_specs=[
            pl.BlockSpec((tm, d), lambda i, j: (i, 0)),
            pl.BlockSpec((1, d), lambda i, j: (0, 0)),
            pl.BlockSpec((d, tn), lambda i, j: (0, j)),
        ],
        out_specs=pl.BlockSpec((tm, tn), lambda i, j: (i, j)),
        scratch_shapes=[pltpu.VMEM((tm, d), _BF16)],
        compiler_params=_params(("arbitrary", "arbitrary")),
        name="inproj",
    )(x2, gain.reshape(1, d), w_bf16)


def _attn_body(sink_ref, q_ref, kp_ref, kc_ref, kn_ref, vp_ref, vc_ref, vn_ref, qg_ref, kg_ref, o_ref,
               *, seq_len, slopes):
    i = pl.program_id(1)
    row = lax.broadcasted_iota(jnp.int32, (BLOCK, 3 * BLOCK), 0)
    col = lax.broadcasted_iota(jnp.int32, (BLOCK, 3 * BLOCK), 1)
    dist = jnp.abs(row + BLOCK - col)
    kpos = (i - 1) * BLOCK + col
    valid = (dist <= WINDOW) & (kpos >= 0) & (kpos < seq_len)
    neg_dist = jnp.where(valid, -dist.astype(_F32), -jnp.inf)

    k3 = jnp.concatenate([kp_ref[...], kc_ref[...], kn_ref[...]], axis=0).astype(_F32)
    v3 = jnp.concatenate([vp_ref[...], vc_ref[...], vn_ref[...]], axis=0)
    qf = q_ref[...].astype(_F32)
    q_gain = qg_ref[...] * (HEAD_DIM ** -0.5)
    k_gain = kg_ref[...]

    for kv in range(N_KV_HEADS):
        kh = k3[:, kv * HEAD_DIM:(kv + 1) * HEAD_DIM]
        kh = (kh * _rms_scale(kh) * k_gain).astype(_BF16)
        q_rows = []
        for g in range(Q_GROUP):
            h = kv * Q_GROUP + g
            qh = qf[:, h * HEAD_DIM:(h + 1) * HEAD_DIM]
            q_rows.append((qh * _rms_scale(qh) * q_gain).astype(_BF16))
        scores = lax.dot_general(jnp.concatenate(q_rows, axis=0), kh, _NT,
                                 preferred_element_type=_F32)
        probs, denoms = [], []
        for g in range(Q_GROUP):
            h = kv * Q_GROUP + g
            s = scores[g * BLOCK:(g + 1) * BLOCK] + slopes[h] * neg_dist
            sink = sink_ref[h]
            m = jnp.maximum(jnp.max(s, axis=-1, keepdims=True), sink)
            p = jnp.exp(s - m)
            denoms.append(jnp.sum(p, axis=-1, keepdims=True) + jnp.exp(sink - m))
            probs.append(p.astype(_BF16))
        pv = jnp.dot(jnp.concatenate(probs, axis=0), v3[:, kv * HEAD_DIM:(kv + 1) * HEAD_DIM],
                     preferred_element_type=_F32)
        for g in range(Q_GROUP):
            h = kv * Q_GROUP + g
            o_ref[:, h * HEAD_DIM:(h + 1) * HEAD_DIM] = (
                pv[g * BLOCK:(g + 1) * BLOCK] / denoms[g]).astype(o_ref.dtype)


def _attention(proj3, q_gain, k_gain, sink):
    b, s, _ = proj3.shape
    nb = s // BLOCK
    aw, kw = N_Q_HEADS * HEAD_DIM, N_KV_HEADS * HEAD_DIM
    k_col, v_col = aw // kw, aw // kw + 1
    slopes = tuple(2.0 ** (-8.0 * (h + 1.0) / N_Q_HEADS) for h in range(N_Q_HEADS))

    def kv_spec(col, shift):
        return pl.BlockSpec((None, BLOCK, kw),
                            lambda bi, i: (bi, jnp.clip(i + shift, 0, nb - 1), col))

    return pl.pallas_call(
        functools.partial(_attn_body, seq_len=s, slopes=slopes),
        out_shape=jax.ShapeDtypeStruct((b, s, aw), _BF16),
        grid=(b, nb),
        in_specs=[
            pl.BlockSpec(memory_space=pltpu.SMEM),
            pl.BlockSpec((None, BLOCK, aw), lambda bi, i: (bi, i, 0)),
            kv_spec(k_col, -1), kv_spec(k_col, 0), kv_spec(k_col, 1),
            kv_spec(v_col, -1), kv_spec(v_col, 0), kv_spec(v_col, 1),
            pl.BlockSpec((1, HEAD_DIM), lambda bi, i: (0, 0)),
            pl.BlockSpec((1, HEAD_DIM), lambda bi, i: (0, 0)),
        ],
        out_specs=pl.BlockSpec((None, BLOCK, aw), lambda bi, i: (bi, i, 0)),
        compiler_params=_params(("arbitrary", "arbitrary")),
        name="attn",
    )(sink.astype(_F32), proj3, proj3, proj3, proj3, proj3, proj3, proj3,
      q_gain.reshape(1, HEAD_DIM).astype(_F32), k_gain.reshape(1, HEAD_DIM).astype(_F32))


def _ssm_matrices(a_re, a_im, log_step, b_re, b_im, c_re, c_im, d_skip):
    n_l, n_c, n_s = SSM_CHUNK, SSM_GROUP, SSM_STATE
    n_g = a_re.shape[1]
    a_re, a_im, b_re, b_im, c_re, c_im = (v.astype(_F32) for v in (a_re, a_im, b_re, b_im, c_re, c_im))
    step = jnp.exp(log_step.astype(_F32))[..., None]
    steps = jnp.arange(n_l + 1, dtype=_F32)[:, None, None, None]
    mag = jnp.exp((a_re * step)[None] * steps)
    ang = (a_im * step)[None] * steps
    p_re, p_im = mag * jnp.cos(ang), mag * jnp.sin(ang)
    num_re, num_im = p_re[1] - 1.0, p_im[1]
    den = a_re * a_re + a_im * a_im
    f_re = ((num_re * a_re + num_im * a_im) / den)[..., None]
    f_im = ((num_im * a_re - num_re * a_im) / den)[..., None]
    bb_re, bb_im = f_re * b_re - f_im * b_im, f_re * b_im + f_im * b_re
    pe_re, pe_im = p_re[:, :, :, None, :], p_im[:, :, :, None, :]
    cp_re = c_re[None] * pe_re - c_im[None] * pe_im
    cp_im = c_re[None] * pe_im + c_im[None] * pe_re

    kern = jnp.sum(cp_re[:n_l, :, :, :, :, None] * bb_re[None, :, :, None, :, :]
                   - cp_im[:n_l, :, :, :, :, None] * bb_im[None, :, :, None, :, :], axis=4).transpose(1, 0, 2, 3, 4)
    gp = LANES // n_c
    n_sg = n_g // gp
    skip = jnp.eye(n_c, dtype=_F32)[None] * d_skip.astype(_F32).reshape(n_g, n_c, 1)
    lagged = jnp.concatenate([kern[1][:0:-1], (kern[0][0] + kern[1][0] + skip)[None], kern[0][1:]], axis=0)
    lag = jnp.arange(n_l)[None, :] - jnp.arange(n_l)[:, None] + (n_l - 1)
    select_lag = (lag[:, :, None] == jnp.arange(2 * n_l - 1)[None, None, :]).astype(_F32)
    m = jnp.einsum('jJt,tshCc->sjcJhC', select_lag, lagged.reshape(2 * n_l - 1, n_sg, gp, n_c, n_c))
    m = m.reshape(n_sg, n_l * n_c, n_l * gp * n_c)

    def state_in(direction, backwards):
        q_re, q_im = p_re[:n_l, direction], p_im[:n_l, direction]
        if backwards:
            q_re, q_im = q_re[::-1], q_im[::-1]
        q_re, q_im = q_re[:, :, None, :], q_im[:, :, None, :]
        t_re, t_im = bb_re[direction].transpose(0, 2, 1)[None], bb_im[direction].transpose(0, 2, 1)[None]
        return q_re * t_re - q_im * t_im, q_re * t_im + q_im * t_re

    w_in_g = jnp.stack(state_in(0, True) + state_in(1, False), axis=3)
    w_in_g = w_in_g.transpose(1, 0, 2, 3, 4)
    out_f = (cp_re[1:n_l + 1, 0], -cp_im[1:n_l + 1, 0])
    out_b = (cp_re[n_l:0:-1, 1], -cp_im[n_l:0:-1, 1])
    w_out_g = jnp.stack([z.transpose(1, 3, 0, 2) for z in out_f + out_b], axis=1)
    lam_g = jnp.stack([p_re[n_l, 0], p_im[n_l, 0], p_re[n_l, 1], p_im[n_l, 1]], axis=1)

    def per_tile(z):
        _, a, b, c, d = z.shape
        return z.reshape(n_sg, gp, a, b, c, d).transpose(0, 2, 3, 4, 1, 5).reshape(n_sg, a * b, c * gp * d)

    w_in, w_out = per_tile(w_in_g), per_tile(w_out_g)
    lam = lam_g.reshape(n_sg, gp, 4, n_s).transpose(0, 2, 1, 3).reshape(n_sg, 4, gp * n_s)
    return m.astype(_BF16), w_in.astype(_BF16), w_out.astype(_BF16), lam


def _expand_block_diagonal(compact_ref, full_scr, block_rows, col_group):
    gp = LANES // SSM_GROUP
    width = compact_ref.shape[1]
    group_of_col = (lax.broadcasted_iota(jnp.int32, (block_rows, width), 1) // col_group) % gp
    zero = jnp.zeros((block_rows, width), compact_ref.dtype)
    for a in range(compact_ref.shape[0] // block_rows):
        block = compact_ref[a * block_rows:(a + 1) * block_rows, :]
        for g in range(gp):
            full_scr[(a * gp + g) * block_rows:(a * gp + g + 1) * block_rows, :] = jnp.where(group_of_col == g, block, zero)


def _ssm_body(x_ref, m_ref, win_ref, wout_ref, lam_ref, o_ref, lhs_scr, st_scr, io_scr, m_scr, win_scr, wout_scr,
              *, n_chunks, n_batch):
    n_l = SSM_CHUNK
    rows = n_batch * n_chunks

    @pl.when(pl.program_id(1) == 0)
    def _():
        _expand_block_diagonal(m_ref, m_scr, SSM_GROUP, SSM_GROUP)
        _expand_block_diagonal(win_ref, win_scr, SSM_GROUP, SSM_STATE)
        _expand_block_diagonal(wout_ref, wout_scr, SSM_STATE, SSM_GROUP)

    io_scr[...] = x_ref[...].astype(_F32)
    for j in range(n_l):
        lhs_scr[:, j * LANES:(j + 1) * LANES] = io_scr[pl.ds(j, rows, stride=n_l), :].astype(_BF16)

    inc = jnp.dot(lhs_scr[...], win_scr[...], preferred_element_type=_F32)
    n_tiles = st_scr.shape[0]
    q = n_tiles // 4
    for t in range(n_tiles):
        st_scr[t] = inc[:, t * LANES:(t + 1) * LANES]
    lam = lam_ref[...]
    coef = [[jnp.broadcast_to(lam[p:p + 1, i * LANES:(i + 1) * LANES], (n_batch, LANES)) for i in range(q)]
            for p in range(4)]

    def step(k, carry):
        rf = pl.ds(k, n_batch, stride=n_chunks)
        rb = pl.ds(n_chunks - 1 - k, n_batch, stride=n_chunks)
        new = list(carry)
        for i in range(q):
            for base, r in ((0, rf), (2 * q, rb)):
                t_re, t_im = base + i, base + q + i
                a_re, a_im = coef[base // q][i], coef[base // q + 1][i]
                s_re, s_im = carry[t_re], carry[t_im]
                d_re, d_im = st_scr[t_re, r, :], st_scr[t_im, r, :]
                st_scr[t_re, r, :] = s_re
                st_scr[t_im, r, :] = s_im
                new[t_re] = a_re * s_re - a_im * s_im + d_re
                new[t_im] = a_re * s_im + a_im * s_re + d_im
        return tuple(new)

    zero = jnp.zeros((n_batch, LANES), _F32)
    lax.fori_loop(0, n_chunks, step, (zero,) * n_tiles, unroll=True)
    states = jnp.concatenate([st_scr[t] for t in range(n_tiles)], axis=1).astype(_BF16)
    y = (jnp.dot(lhs_scr[...], m_scr[...], preferred_element_type=_F32)
         + jnp.dot(states, wout_scr[...], preferred_element_type=_F32))
    y = _gelu(y)
    for j in range(n_l):
        io_scr[pl.ds(j, rows, stride=n_l), :] = y[:, j * LANES:(j + 1) * LANES]
    o_ref[...] = io_scr[...].astype(o_ref.dtype)


def _ssm(proj, u_col, m, w_in, w_out, lam, n_batch_total, seq_len):
    t = proj.shape[0]
    n_sg, _, width = m.shape
    bt = 4
    n_chunks = seq_len // SSM_CHUNK
    rows = bt * n_chunks
    n_state = w_in.shape[2]
    n_tiles = n_state // LANES
    col0 = u_col // LANES
    compact = lambda arr: pl.BlockSpec((None,) + arr.shape[1:], lambda s, i: (s, 0, 0),
                                       pipeline_mode=pl.Buffered(1))
    return pl.pallas_call(
        functools.partial(_ssm_body, n_chunks=n_chunks, n_batch=bt),
        out_shape=jax.ShapeDtypeStruct((t, n_sg * LANES), _BF16),
        grid=(n_sg, n_batch_total // bt),
        in_specs=[
            pl.BlockSpec((bt * seq_len, LANES), lambda s, i: (i, col0 + s)),
            compact(m), compact(w_in), compact(w_out), compact(lam),
        ],
        out_specs=pl.BlockSpec((bt * seq_len, LANES), lambda s, i: (i, s)),
        scratch_shapes=[pltpu.VMEM((rows, width), _BF16), pltpu.VMEM((n_tiles, rows, LANES), _F32),
                        pltpu.VMEM((bt * seq_len, LANES), _F32),
                        pltpu.VMEM((width, width), _BF16), pltpu.VMEM((width, n_state), _BF16),
                        pltpu.VMEM((n_state, width), _BF16)],
        compiler_params=_params(("arbitrary", "arbitrary")),
        name="ssm",
    )(proj, m, w_in, w_out, lam)


def _merge_body(attn_ref, yact_ref, ga_ref, gb_ref, wo_ref, wa_ref, wb_ref, o_ref):
    y_a = jnp.dot(attn_ref[...], wo_ref[...], preferred_element_type=_F32)
    yact = yact_ref[...]
    glu = (jnp.dot(yact, wa_ref[...], preferred_element_type=_F32)
           * jax.nn.sigmoid(jnp.dot(yact, wb_ref[...], preferred_element_type=_F32)))
    o_ref[...] = (jax.nn.sigmoid(ga_ref[...].astype(_F32)) * y_a
                  + jax.nn.sigmoid(gb_ref[...].astype(_F32)) * glu).astype(o_ref.dtype)


def _merge(attn2, yact2, proj, gate_a_col, gate_b_col, wo, wa, wb):
    t, aw = attn2.shape
    sw = yact2.shape[1]
    d = wo.shape[1]
    tm, tn = min(512, t), d
    ga0, gb0 = gate_a_col // tn, gate_b_col // tn
    once = pl.Buffered(1)
    return pl.pallas_call(
        _merge_body,
        out_shape=jax.ShapeDtypeStruct((t, d), _BF16),
        grid=(t // tm, d // tn),
        in_specs=[
            pl.BlockSpec((tm, aw), lambda i, j: (i, 0)),
            pl.BlockSpec((tm, sw), lambda i, j: (i, 0)),
            pl.BlockSpec((tm, tn), lambda i, j: (i, ga0 + j)),
            pl.BlockSpec((tm, tn), lambda i, j: (i, gb0 + j)),
            pl.BlockSpec((aw, tn), lambda i, j: (0, j), pipeline_mode=once),
            pl.BlockSpec((sw, tn), lambda i, j: (0, j), pipeline_mode=once),
            pl.BlockSpec((sw, tn), lambda i, j: (0, j), pipeline_mode=once),
        ],
        out_specs=pl.BlockSpec((tm, tn), lambda i, j: (i, j)),
        compiler_params=_params(("arbitrary", "arbitrary")),
        name="merge",
    )(attn2, yact2, proj, proj, wo, wa, wb)


def _outproj_body(x_ref, m_ref, w_ref, g_ref, wq_ref, x1_ref, h2_ref, q_ref):
    x1 = x_ref[...] + jnp.dot(m_ref[...], w_ref[...], preferred_element_type=_F32)
    x1_ref[...] = x1
    h2 = (x1 * _rms_scale(x1) * g_ref[...]).astype(h2_ref.dtype)
    h2_ref[...] = h2
    q_ref[...] = jnp.dot(h2, wq_ref[...], preferred_element_type=_F32).astype(q_ref.dtype)


def _outproj(x2, merged, w_out, gain, wq):
    t, d = x2.shape
    qw = wq.shape[1]
    tm = min(512, t)
    rows = lambda width: pl.BlockSpec((tm, width), lambda i: (i, 0))
    whole = lambda arr: pl.BlockSpec(arr.shape, lambda i: (0, 0))
    return pl.pallas_call(
        _outproj_body,
        out_shape=(jax.ShapeDtypeStruct((t, d), _F32), jax.ShapeDtypeStruct((t, d), _BF16),
                   jax.ShapeDtypeStruct((t, qw), _BF16)),
        grid=(t // tm,),
        in_specs=[rows(d), rows(d), whole(w_out), pl.BlockSpec((1, d), lambda i: (0, 0)), whole(wq)],
        out_specs=(rows(d), rows(d), rows(qw)),
        compiler_params=_params(("arbitrary",)),
        name="outproj",
    )(x2, merged, w_out, gain.reshape(1, d), wq)


_CAND_WIDE = 4
_CAND_TAIL = tuple((p1, p2) for p1 in range(_CAND_WIDE, PEER_TOPK) for p2 in range(PEER_TOPK)
                   if (p1 + 1) * (p2 + 1) <= PEER_TOPK)
_SUBLANES = 8


def _top_rows(vals, k, exact, want_rank):
    n = vals.shape[0]
    row = lax.broadcasted_iota(jnp.int32, vals.shape, 0).astype(_F32) if exact else None
    rank = jnp.full(vals.shape, float(k), _F32) if want_rank else None
    work = vals
    tops = []
    for r in range(k):
        best = jnp.max(work, axis=0, keepdims=True)
        hit = work == best
        if exact:
            first = jnp.min(jnp.where(hit, row, float(n)), axis=0, keepdims=True)
            hit = row == first
        if want_rank:
            rank = jnp.where(hit, float(r), rank)
        work = jnp.where(hit, -jnp.inf, work)
        tops.append(best)
    return jnp.concatenate(tops, axis=0), rank, work


def _extracted(left):
    return jnp.sum(jnp.where(left == -jnp.inf, 1.0, 0.0), axis=0, keepdims=True)


def _rank_keys(s1, s2, exact):
    k = PEER_TOPK
    top1, rank1, left1 = _top_rows(s1, k, exact, want_rank=exact)
    top2, rank2, left2 = _top_rows(s2, k, exact, want_rank=True)
    e2 = jnp.exp(s2 - top2[0:1])
    bad = None if exact else jnp.abs(_extracted(left1) - k) + jnp.abs(_extracted(left2) - k)
    return top1, top2, rank1, rank2, e2, bad


def _rank_pairs(s1, top1, top2, rank1, exact):
    k = PEER_TOPK
    n_pad = -len(_CAND_TAIL) % _SUBLANES
    pad = [jnp.full_like(top1[0:1], -jnp.inf)] * n_pad
    cand = jnp.concatenate([top1[p:p + 1] + top2 for p in range(_CAND_WIDE)]
                           + [top1[p1:p1 + 1] + top2[p2:p2 + 1] for p1, p2 in _CAND_TAIL] + pad, axis=0)
    top_c, _, left_c = _top_rows(cand, k, exact, want_rank=False)
    chosen = jnp.where(left_c == -jnp.inf, 1.0, 0.0)
    per_rank = [jnp.sum(chosen[p * k:(p + 1) * k], axis=0, keepdims=True) for p in range(_CAND_WIDE)]
    for p1 in range(_CAND_WIDE, k):
        rows = [_CAND_WIDE * k + i for i, (a, _) in enumerate(_CAND_TAIL) if a == p1]
        per_rank.append(jnp.sum(chosen[rows[0]:rows[-1] + 1], axis=0, keepdims=True))
    count = jnp.zeros(s1.shape, _F32)
    for p in range(k):
        count = jnp.where((rank1 == p) if exact else (s1 == top1[p:p + 1]), per_rank[p], count)
    z = jnp.sum(jnp.exp(top_c - top_c[0:1]), axis=0, keepdims=True)
    coef = jnp.exp(s1 - top1[0:1]) / z
    bad = None if exact else jnp.abs(_extracted(left_c) - (k + n_pad))
    return count, coef, bad


def _select_body(q_ref, k1_ref, k2_ref, cnt_ref, coef_ref, rank_ref, e2_ref, s1_scr, s2_scr):
    q = q_ref[...]
    s1_scr[...] = lax.dot_general(k1_ref[...], q[:, :PEER_HALF], _NT, preferred_element_type=_F32)
    s2_scr[...] = lax.dot_general(k2_ref[...], q[:, PEER_HALF:], _NT, preferred_element_type=_F32)
    n_tiles = s1_scr.shape[1] // LANES

    def lanes_of(i):
        return pl.ds(pl.multiple_of(i * LANES, LANES), LANES)

    def keys_stage(i, exact):
        lanes = lanes_of(i)
        top1, top2, rank1, rank2, e2, bad = _rank_keys(s1_scr[:, lanes], s2_scr[:, lanes], exact)
        rank_ref[:, lanes] = rank2.astype(rank_ref.dtype)
        e2_ref[:, lanes] = e2.astype(e2_ref.dtype)
        return top1, top2, rank1, bad

    def pairs_stage(i, top1, top2, rank1, exact):
        lanes = lanes_of(i)
        count, coef, bad = _rank_pairs(s1_scr[:, lanes], top1, top2, rank1, exact)
        cnt_ref[:, lanes] = count
        coef_ref[:, lanes] = coef
        return bad

    top1, top2, _, bad = keys_stage(0, exact=False)

    def skewed(i, carry):
        top1, top2, bad = carry
        nxt1, nxt2, _, bad_keys = keys_stage(i + 1, exact=False)
        bad_pairs = pairs_stage(i, top1, top2, None, exact=False)
        return nxt1, nxt2, jnp.maximum(bad, jnp.maximum(bad_keys, bad_pairs))

    top1, top2, bad = lax.fori_loop(0, n_tiles - 1, skewed, (top1, top2, bad), unroll=True)
    bad = jnp.maximum(bad, pairs_stage(n_tiles - 1, top1, top2, None, exact=False))

    @pl.when(jnp.max(bad) > 0.0)
    def _():
        def exact_tile(i, carry):
            top1, top2, rank1, _ = keys_stage(i, exact=True)
            pairs_stage(i, top1, top2, rank1, exact=True)
            return carry

        lax.fori_loop(0, n_tiles, exact_tile, 0)


def _select(q, keys1, keys2):
    t = q.shape[0]
    tm = min(512, t)
    qd = 2 * PEER_HALF
    out = jax.ShapeDtypeStruct((PEER_HEADS, N_KEYS, t), _F32)
    out16 = jax.ShapeDtypeStruct((PEER_HEADS, N_KEYS, t), _BF16)
    spec = pl.BlockSpec((None, N_KEYS, tm), lambda i, h: (h, 0, i))
    return pl.pallas_call(
        _select_body,
        out_shape=(out, out, out16, out16),
        grid=(t // tm, PEER_HEADS),
        in_specs=[
            pl.BlockSpec((tm, qd), lambda i, h: (i, h)),
            pl.BlockSpec((None, N_KEYS, PEER_HALF), lambda i, h: (h, 0, 0)),
            pl.BlockSpec((None, N_KEYS, PEER_HALF), lambda i, h: (h, 0, 0)),
        ],
        out_specs=(spec, spec, spec, spec),
        scratch_shapes=[pltpu.VMEM((N_KEYS, tm), _F32), pltpu.VMEM((N_KEYS, tm), _F32)],
        compiler_params=_params(("arbitrary", "arbitrary")),
        name="select",
    )(q, keys1, keys2)


def _peer_body(h2_ref, down_ref, upt_ref, cnt_ref, coef_ref, rank_ref, e2_ref, x1_ref, o_ref,
               acc_scr, wg_scr, *, key_rows):
    j = pl.program_id(1)

    @pl.when(j == 0)
    def _():
        acc_scr[...] = jnp.zeros_like(acc_scr)

    act = lax.dot_general(down_ref[...], h2_ref[...], _NT, preferred_element_type=_F32)
    gact = _gelu(act).astype(_BF16)
    block = (N_KEYS, gact.shape[1])
    zero = jnp.zeros(block, _BF16)
    for a in range(key_rows):
        rows = slice(a * N_KEYS, (a + 1) * N_KEYS)
        w = None
        for h in range(PEER_HEADS):
            cnt = jnp.broadcast_to(cnt_ref[h, a:a + 1, :].astype(_BF16), block)
            coef = jnp.broadcast_to(coef_ref[h, a:a + 1, :].astype(_BF16), block)
            term = jnp.where(rank_ref[h] < cnt, e2_ref[h], zero) * coef
            w = term if w is None else w + term
        wg_scr[rows, :] = w * gact[rows, :]
    acc_scr[...] += jnp.dot(upt_ref[...], wg_scr[...], preferred_element_type=_F32)

    @pl.when(j == pl.num_programs(1) - 1)
    def _():
        o_ref[...] = x1_ref[...] + acc_scr[...].T


def _peer(h2, down, up_t, count, coef, rank2, e2, x1):
    t, d = h2.shape
    n_e = down.shape[0]
    tm, te = min(512, t), 1024
    key_rows = te // N_KEYS
    once = pl.Buffered(1)
    return pl.pallas_call(
        functools.partial(_peer_body, key_rows=key_rows),
        out_shape=jax.ShapeDtypeStruct((t, d), _F32),
        grid=(t // tm, n_e // te),
        in_specs=[
            pl.BlockSpec((tm, d), lambda i, j: (i, 0), pipeline_mode=once),
            pl.BlockSpec((te, d), lambda i, j: (j, 0)),
            pl.BlockSpec((d, te), lambda i, j: (0, j)),
            pl.BlockSpec((PEER_HEADS, key_rows, tm), lambda i, j: (0, j, i)),
            pl.BlockSpec((PEER_HEADS, key_rows, tm), lambda i, j: (0, j, i)),
            pl.BlockSpec((PEER_HEADS, N_KEYS, tm), lambda i, j: (0, 0, i), pipeline_mode=once),
            pl.BlockSpec((PEER_HEADS, N_KEYS, tm), lambda i, j: (0, 0, i), pipeline_mode=once),
            pl.BlockSpec((tm, d), lambda i, j: (i, 0), pipeline_mode=once),
        ],
        out_specs=pl.BlockSpec((tm, d), lambda i, j: (i, 0)),
        scratch_shapes=[pltpu.VMEM((d, tm), _F32), pltpu.VMEM((te, tm), _BF16)],
        compiler_params=_params(("arbitrary", "arbitrary")),
        name="peer",
    )(h2, down, up_t, count, coef, rank2, e2, x1)


def _layer(x, mix_g, w_in, q_g, k_g, sink, w_attn_o, a_re, a_im, log_step, b_re, b_im, c_re, c_im, d_skip,
           glu_a, glu_b, w_out, ffn_g, wq, keys1, keys2, down, up):
    b, s, d = x.shape
    t = b * s
    aw, kw = N_Q_HEADS * HEAD_DIM, N_KV_HEADS * HEAD_DIM
    sw = d_skip.shape[0]
    u_col = aw + 2 * kw
    ga_col = u_col + sw
    gb_col = ga_col + d
    x2 = x.reshape(t, d)

    proj = _inproj(x2, mix_g, w_in.astype(_BF16))
    attn = _attention(proj.reshape(b, s, -1), q_g, k_g, sink).reshape(t, aw)

    m, s_in, s_out, lam = _ssm_matrices(a_re, a_im, log_step, b_re, b_im, c_re, c_im, d_skip)
    yact = _ssm(proj, u_col, m, s_in, s_out, lam, b, s)

    merged = _merge(attn, yact, proj, ga_col, gb_col,
                    w_attn_o.astype(_BF16), glu_a.astype(_BF16), glu_b.astype(_BF16))
    x1, h2, q = _outproj(x2, merged, w_out.astype(_BF16), ffn_g, wq.astype(_BF16))

    count, coef, rank2, e2 = _select(q, keys1.astype(_BF16), keys2.astype(_BF16))
    out = _peer(h2, down.astype(_BF16), up.T.astype(_BF16), count, coef, rank2, e2, x1)
    return out.reshape(b, s, d)


def kernel(x, mix_norm_g, w_in, q_norm_g, k_norm_g, attn_sink, w_attn_o, ssm_a_re, ssm_a_im, ssm_log_step,
           ssm_b_re, ssm_b_im, ssm_c_re, ssm_c_im, ssm_d, glu_w_a, glu_w_b, w_out, ffn_norm_g, peer_w_query,
           peer_sub_keys_1, peer_sub_keys_2, peer_down, peer_up):
    for l in range(mix_norm_g.shape[0]):
        x = _layer(x, mix_norm_g[l], w_in[l], q_norm_g[l], k_norm_g[l], attn_sink[l], w_attn_o[l],
                   ssm_a_re[l], ssm_a_im[l], ssm_log_step[l], ssm_b_re[l], ssm_b_im[l], ssm_c_re[l],
                   ssm_c_im[l], ssm_d[l], glu_w_a[l], glu_w_b[l], w_out[l], ffn_norm_g[l], peer_w_query[l],
                   peer_sub_keys_1[l], peer_sub_keys_2[l], peer_down[l], peer_up[l])
    return x
```
